```python
import jax, jax.numpy as jnp
from jax import lax
import numpy as np

D_MODEL = 1024
BATCH = 2
SEQ = 8192
DEPTH = 2
DEC_BATCH = 16
DEC_SEQ = 32
PAST_LEN = 1024

CHUNK = 64
Q_BLOCK = 128
MIX_WIDTH = D_MODEL // 2
HEAD_DIM = 64
ATT_HEADS = MIX_WIDTH // HEAD_DIM
KV_HEADS = ATT_HEADS // 4
IDX_HEADS = 4
IDX_DIM = 64
TOPK_MAX = 256
ROPE_THETA = 10000.0
SSM_INNER = MIX_WIDTH
SSM_HEAD_DIM = 64
SSM_HEADS = SSM_INNER // SSM_HEAD_DIM
SSM_GROUPS = 2
SSM_STATE = 64
SSM_CONV = 4
SSM_CONV_DIM = SSM_INNER + 2 * SSM_GROUPS * SSM_STATE
SSD_CHUNK = CHUNK
LRU_WIDTH = MIX_WIDTH
LRU_BLOCKS = 8
LRU_BW = LRU_WIDTH // LRU_BLOCKS
LRU_CONV = 4
LRU_C = 8.0
D_FF = ((8 * D_MODEL // 3 + 127) // 128) * 128
N_EXPERTS = 8
TOP_K = 2
N_DENSE = (DEPTH + 1) // 2
N_MOE = DEPTH // 2
NORM_EPS = 1e-6
IN_SIZES = (ATT_HEADS * HEAD_DIM, KV_HEADS * HEAD_DIM, KV_HEADS * HEAD_DIM, IDX_HEADS * IDX_DIM, IDX_DIM, IDX_HEADS,
            SSM_INNER, SSM_CONV_DIM, SSM_HEADS, LRU_WIDTH, LRU_WIDTH, 3 * D_MODEL)
IN_WIDTH = sum(IN_SIZES)

kernel_name = 'hybrid_dsa_ssd_rglru_stream_step'


def _split_cols(h, sizes):
    outs, start = [], 0
    for s in sizes:
        outs.append(h[..., start:start + s])
        start += s
    return outs


def rmsnorm(x, g):
    xf = x.astype(jnp.float32)
    y = xf * lax.rsqrt(jnp.mean(xf * xf, axis=-1, keepdims=True) + NORM_EPS)
    return (y * g.astype(jnp.float32)).astype(x.dtype)


def rope(x, pos):
    half = x.shape[-1] // 2
    inv = 1.0 / (ROPE_THETA ** (jnp.arange(half, dtype=jnp.float32) / half))
    ang = pos.astype(jnp.float32)[:, None] * inv[None, :]
    cos = jnp.cos(ang)[:, None, :]
    sin = jnp.sin(ang)[:, None, :]
    xf = x.astype(jnp.float32)
    x1, x2 = xf[..., :half], xf[..., half:]
    return jnp.concatenate([x1 * cos - x2 * sin, x2 * cos + x1 * sin], axis=-1).astype(x.dtype)


def causal_conv(x, buf, w, b):
    width = w.shape[0]
    length = x.shape[1]
    xp = jnp.concatenate([buf.astype(x.dtype), x], axis=1)
    y = b
    for j in range(width):
        y = y + xp[:, j:j + length] * w[j]
    return y, xp[:, length:]


def dsa_block(q, qi, wi, q_pos, k, v, ki, k_pos, topk):
    rel = jax.nn.relu(jnp.einsum('bthd,bsd->bths', qi.astype(jnp.float32), ki.astype(jnp.float32)))
    score = jnp.einsum('bths,bth->bts', rel, wi.astype(jnp.float32))
    adm = (k_pos[None, :] // CHUNK) <= (q_pos[:, None] // CHUNK)
    score = jnp.where(adm[None], score, -jnp.inf)
    _, idx = lax.top_k(score, topk)
    valid = (k_pos[idx] // CHUNK) <= (q_pos[None, :, None] // CHUNK)
    gather = jax.vmap(lambda arr, ix: arr[ix])
    kg = gather(k, idx)
    vg = gather(v, idx)
    bsz, t, nh, dh = q.shape
    qg = q.reshape(bsz, t, KV_HEADS, nh // KV_HEADS, dh).astype(jnp.float32)
    logits = jnp.einsum('btkgd,btjkd->btkgj', qg, kg.astype(jnp.float32)) * (dh ** -0.5)
    logits = jnp.where(valid[:, :, None, None, :], logits, -jnp.inf)
    p = jax.nn.softmax(logits, axis=-1)
    out = jnp.einsum('btkgj,btjkd->btkgd', p, vg.astype(jnp.float32))
    return out.reshape(bsz, t, nh * dh).astype(q.dtype)


def dsa_attention(q, qi, wi, q_pos, k, v, ki, k_pos, topk):
    bsz, t = q.shape[0], q.shape[1]
    if t <= Q_BLOCK:
        return dsa_block(q, qi, wi, q_pos, k, v, ki, k_pos, topk)
    nb = t // Q_BLOCK

    def blocks(a):
        return jnp.moveaxis(a.reshape(bsz, nb, Q_BLOCK, *a.shape[2:]), 1, 0)

    outs = lax.map(lambda a: dsa_block(a[0], a[1], a[2], a[3], k, v, ki, k_pos, topk),
                   (blocks(q), blocks(qi), blocks(wi), q_pos.reshape(nb, Q_BLOCK)))
    return jnp.moveaxis(outs, 0, 1).reshape(bsz, t, -1)


def _pad_time(a, pad):
    widths = [(0, 0)] * a.ndim
    widths[1] = (0, pad)
    return jnp.pad(a, widths)


def ssd_scan(x, dt, a_neg, b_in, c_in, h0):
    bsz, length, nh, hp = x.shape
    pad = (-length) % SSD_CHUNK
    if pad:
        x, dt, b_in, c_in = [_pad_time(t, pad) for t in (x, dt, b_in, c_in)]
    nc = (length + pad) // SSD_CHUNK
    rep = nh // b_in.shape[2]
    n = b_in.shape[-1]
    bh = jnp.repeat(b_in.astype(jnp.float32), rep, axis=2).reshape(bsz, nc, SSD_CHUNK, nh, n)
    ch = jnp.repeat(c_in.astype(jnp.float32), rep, axis=2).reshape(bsz, nc, SSD_CHUNK, nh, n)
    xdt = (x.astype(jnp.float32) * dt[..., None]).reshape(bsz, nc, SSD_CHUNK, nh, hp)
    a_cum = jnp.cumsum((dt * a_neg).reshape(bsz, nc, SSD_CHUNK, nh).transpose(0, 3, 1, 2), axis=-1)
    diff = a_cum[..., :, None] - a_cum[..., None, :]
    causal = jnp.tril(jnp.ones((SSD_CHUNK, SSD_CHUNK), dtype=bool))
    decay_in = jnp.exp(jnp.where(causal, diff, -jnp.inf))
    cb = jnp.einsum('bclhn,bcshn->bhcls', ch, bh) * decay_in
    y_diag = jnp.einsum('bhcls,bcshp->bclhp', cb, xdt)
    decay_to_end = jnp.exp(a_cum[..., -1:] - a_cum)
    chunk_states = jnp.einsum('bclhn,bhcl,bclhp->cbhpn', bh, decay_to_end, xdt)
    chunk_decay = jnp.exp(a_cum[..., -1]).transpose(2, 0, 1)

    def step(h, inp):
        dec, st = inp
        return dec[..., None, None] * h + st, h

    h_final, h_enter = lax.scan(step, h0.astype(jnp.float32), (chunk_decay, chunk_states))
    y_off = jnp.einsum('bclhn,cbhpn,bhcl->bclhp', ch, h_enter, jnp.exp(a_cum))
    y = (y_diag + y_off).reshape(bsz, nc * SSD_CHUNK, nh, hp)[:, :length]
    return y, h_final


def ssm_branch(z, xbc, dt_raw, conv_buf, h0, conv_w, conv_b, dt_bias, a_log, d_skip, norm_w):
    xbc_c, new_buf = causal_conv(xbc, conv_buf, conv_w, conv_b)
    xbc_c = jax.nn.silu(xbc_c)
    xs, bm, cm = _split_cols(xbc_c, (SSM_INNER, SSM_GROUPS * SSM_STATE, SSM_GROUPS * SSM_STATE))
    bsz, length, _ = xs.shape
    xs = xs.reshape(bsz, length, SSM_HEADS, SSM_HEAD_DIM)
    bm = bm.reshape(bsz, length, SSM_GROUPS, SSM_STATE)
    cm = cm.reshape(bsz, length, SSM_GROUPS, SSM_STATE)
    dt = jax.nn.softplus(dt_raw.astype(jnp.float32) + dt_bias.astype(jnp.float32))
    a_neg = -jnp.exp(a_log.astype(jnp.float32))
    y, h_new = ssd_scan(xs, dt, a_neg, bm, cm, h0)
    y = y + d_skip.astype(jnp.float32)[:, None] * xs.astype(jnp.float32)
    y = y.reshape(bsz, length, SSM_INNER) * jax.nn.silu(z.astype(jnp.float32))
    yg = y.reshape(bsz, length, SSM_GROUPS, SSM_INNER // SSM_GROUPS)
    yg = yg * lax.rsqrt(jnp.mean(yg * yg, axis=-1, keepdims=True) + NORM_EPS)
    y = yg.reshape(bsz, length, SSM_INNER) * norm_w.astype(jnp.float32)
    return y.astype(z.dtype), h_new.astype(h0.dtype), new_buf


def lru_branch(xl, gate, conv_buf, h0, conv_w, conv_b, wa, ba, wx, bx, lam):
    xc, new_buf = causal_conv(xl, conv_buf, conv_w, conv_b)
    bsz, length, _ = xc.shape
    xb = xc.reshape(bsz, length, LRU_BLOCKS, LRU_BW)
    r = jax.nn.sigmoid(jnp.einsum('blki,kij->blkj', xb, wa).reshape(bsz, length, LRU_WIDTH) + ba)
    i = jax.nn.sigmoid(jnp.einsum('blki,kij->blkj', xb, wx).reshape(bsz, length, LRU_WIDTH) + bx)
    log_a = -LRU_C * jax.nn.softplus(-lam.astype(jnp.float32)) * r.astype(jnp.float32)
    a = jnp.exp(log_a)
    u = jnp.sqrt(-jnp.expm1(2.0 * log_a)) * (i * xc).astype(jnp.float32)
    u = u.at[:, 0].add(a[:, 0] * h0.astype(jnp.float32))

    def comb(e1, e2):
        a1, b1 = e1
        a2, b2 = e2
        return a1 * a2, a2 * b1 + b2

    _, h = lax.associative_scan(comb, (a, u), axis=1)
    y = h * jax.nn.gelu(gate.astype(jnp.float32), approximate=True)
    return y.astype(xl.dtype), h[:, -1].astype(h0.dtype), new_buf


def token_mixers(xn, q_pos, kv_past, conv_ssm, h_ssm, conv_lru, h_lru, W, layer, topk):
    bsz, length, _ = xn.shape
    proj = jnp.einsum('bld,de->ble', xn, W['w_in'][layer])
    q, k, v, qi, ki, wi, z, xbc, dt_raw, xl, gl, gates = _split_cols(proj, IN_SIZES)
    q = rope(q.reshape(bsz, length, ATT_HEADS, HEAD_DIM), q_pos)
    k = rope(k.reshape(bsz, length, KV_HEADS, HEAD_DIM), q_pos)
    v = v.reshape(bsz, length, KV_HEADS, HEAD_DIM)
    qi = rope(qi.reshape(bsz, length, IDX_HEADS, IDX_DIM), q_pos)
    ki = rope(ki.reshape(bsz, length, 1, IDX_DIM), q_pos)[:, :, 0]
    if kv_past is None:
        k_all, v_all, ki_all, k_pos = k, v, ki, q_pos
    else:
        pk, pv, pki = kv_past
        k_all = jnp.concatenate([pk.astype(k.dtype), k], axis=1)
        v_all = jnp.concatenate([pv.astype(v.dtype), v], axis=1)
        ki_all = jnp.concatenate([pki.astype(ki.dtype), ki], axis=1)
        k_pos = jnp.arange(pk.shape[1] + length)
    att = dsa_attention(q, qi, wi, q_pos, k_all, v_all, ki_all, k_pos, topk)
    y_ssm, h_ssm_new, conv_ssm_new = ssm_branch(
        z, xbc, dt_raw, conv_ssm, h_ssm, W['ssm_conv_w'][layer], W['ssm_conv_b'][layer],
        W['ssm_dt_bias'][layer], W['ssm_a_log'][layer], W['ssm_d'][layer], W['ssm_norm'][layer])
    y_lru, h_lru_new, conv_lru_new = lru_branch(
        xl, gl, conv_lru, h_lru, W['lru_conv_w'][layer], W['lru_conv_b'][layer], W['lru_wa'][layer],
        W['lru_ba'][layer], W['lru_wx'][layer], W['lru_bx'][layer], W['lru_lambda'][layer])
    g = jax.nn.sigmoid(gates.astype(jnp.float32)).reshape(bsz, length, 3, D_MODEL)
    merged = (g[:, :, 0] * (att @ W['w_att_out'][layer]).astype(jnp.float32)
              + g[:, :, 1] * (y_ssm @ W['w_ssm_out'][layer]).astype(jnp.float32)
              + g[:, :, 2] * (y_lru @ W['w_lru_out'][layer]).astype(jnp.float32))
    out = merged.astype(xn.dtype) @ W['w_o'][layer]
    return out, (k, v, ki, h_ssm_new, conv_ssm_new, h_lru_new, conv_lru_new)


def swiglu(x, w1, w3, w2):
    return (jax.nn.silu(x @ w1) * (x @ w3)) @ w2


def moe_swiglu(x, router, w1, w3, w2):
    logits = (x @ router).astype(jnp.float32)
    top_v, top_i = lax.top_k(logits, TOP_K)
    top_w = jax.nn.softmax(top_v, axis=-1)
    gate = jnp.sum(jax.nn.one_hot(top_i, N_EXPERTS, dtype=jnp.float32) * top_w[..., None], axis=-2)
    out = jnp.zeros(x.shape, jnp.float32)
    for e in range(N_EXPERTS):
        out = out + gate[..., e:e + 1] * swiglu(x, w1[e], w3[e], w2[e]).astype(jnp.float32)
    return out.astype(x.dtype)


def run_trunk(x, q_pos, past, W, topk):
    bsz = x.shape[0]
    collected = [[] for _ in range(7)]
    for layer in range(DEPTH):
        if past is None:
            kv_past = None
            conv_ssm = jnp.zeros((bsz, SSM_CONV - 1, SSM_CONV_DIM), x.dtype)
            h_ssm = jnp.zeros((bsz, SSM_HEADS, SSM_HEAD_DIM, SSM_STATE), x.dtype)
            conv_lru = jnp.zeros((bsz, LRU_CONV - 1, LRU_WIDTH), x.dtype)
            h_lru = jnp.zeros((bsz, LRU_WIDTH), x.dtype)
        else:
            kv_past = (past[0][layer], past[1][layer], past[2][layer])
            h_ssm, conv_ssm, h_lru, conv_lru = past[3][layer], past[4][layer], past[5][layer], past[6][layer]
        xn = rmsnorm(x, W['norm_mix'][layer])
        mix, st = token_mixers(xn, q_pos, kv_past, conv_ssm, h_ssm, conv_lru, h_lru, W, layer, topk)
        x = x + mix
        xn = rmsnorm(x, W['norm_ffn'][layer])
        j = layer // 2
        if layer % 2 == 0:
            x = x + swiglu(xn, W['ffn_w1'][j], W['ffn_w3'][j], W['ffn_w2'][j])
        else:
            x = x + moe_swiglu(xn, W['moe_router'][j], W['moe_w1'][j], W['moe_w3'][j], W['moe_w2'][j])
        for lst, s in zip(collected, st):
            lst.append(s)
    y = rmsnorm(x, W['norm_final'])
    return y, [jnp.stack(lst, axis=0) for lst in collected]


def setup_inputs(seed: int = 0) -> dict:
    key = jax.random.key(seed)
    k = jax.random.split(key, 40)
    f32 = jnp.float32

    def nrm(i, shape, scale):
        return scale * jax.random.normal(k[i], shape, f32)

    dt0 = jnp.exp(jax.random.uniform(k[14], (DEPTH, SSM_HEADS), f32, np.log(1e-3), np.log(1e-1)))
    a_c = jax.random.uniform(k[26], (DEPTH, LRU_WIDTH), f32, 0.9, 0.999)
    s = a_c ** (1.0 / LRU_C)
    return {
        'x_prompt': nrm(0, (BATCH, SEQ, D_MODEL), 1.0),
        'x_sample': nrm(1, (DEC_BATCH, DEC_SEQ, D_MODEL), 1.0),
        'cache_k': nrm(2, (DEPTH, DEC_BATCH, PAST_LEN, KV_HEADS, HEAD_DIM), 1.0),
        'cache_v': nrm(3, (DEPTH, DEC_BATCH, PAST_LEN, KV_HEADS, HEAD_DIM), 1.0),
        'cache_kidx': nrm(4, (DEPTH, DEC_BATCH, PAST_LEN, IDX_DIM), 1.0),
        'state_ssm': nrm(5, (DEPTH, DEC_BATCH, SSM_HEADS, SSM_HEAD_DIM, SSM_STATE), 0.5),
        'state_ssm_conv': nrm(6, (DEPTH, DEC_BATCH, SSM_CONV - 1, SSM_CONV_DIM), 1.0),
        'state_lru': nrm(7, (DEPTH, DEC_BATCH, LRU_WIDTH), 0.5),
        'state_lru_conv': nrm(8, (DEPTH, DEC_BATCH, LRU_CONV - 1, LRU_WIDTH), 1.0),
        'norm_mix': 1.0 + nrm(9, (DEPTH, D_MODEL), 0.05),
        'norm_ffn': 1.0 + nrm(10, (DEPTH, D_MODEL), 0.05),
        'norm_final': 1.0 + nrm(11, (D_MODEL,), 0.05),
        'w_in': nrm(12, (DEPTH, D_MODEL, IN_WIDTH), D_MODEL ** -0.5),
        'ssm_conv_w': nrm(13, (DEPTH, SSM_CONV, SSM_CONV_DIM), SSM_CONV ** -0.5),
        'ssm_conv_b': nrm(15, (DEPTH, SSM_CONV_DIM), 0.01),
        'ssm_dt_bias': dt0 + jnp.log(-jnp.expm1(-dt0)),
        'ssm_a_log': jnp.log(jax.random.uniform(k[16], (DEPTH, SSM_HEADS), f32, 1.0, 16.0)),
        'ssm_d': 1.0 + nrm(17, (DEPTH, SSM_HEADS), 0.1),
        'ssm_norm': 1.0 + nrm(18, (DEPTH, SSM_INNER), 0.05),
        'lru_conv_w': nrm(19, (DEPTH, LRU_CONV, LRU_WIDTH), LRU_CONV ** -0.5),
        'lru_conv_b': nrm(20, (DEPTH, LRU_WIDTH), 0.01),
        'lru_wa': nrm(21, (DEPTH, LRU_BLOCKS, LRU_BW, LRU_BW), LRU_BW ** -0.5),
        'lru_ba': nrm(22, (DEPTH, LRU_WIDTH), 0.01),
        'lru_wx': nrm(23, (DEPTH, LRU_BLOCKS, LRU_BW, LRU_BW), LRU_BW ** -0.5),
        'lru_bx': nrm(24, (DEPTH, LRU_WIDTH), 0.01),
        'lru_lambda': jnp.log(s) - jnp.log1p(-s),
        'w_att_out': nrm(27, (DEPTH, ATT_HEADS * HEAD_DIM, D_MODEL), (ATT_HEADS * HEAD_DIM) ** -0.5),
        'w_ssm_out': nrm(28, (DEPTH, SSM_INNER, D_MODEL), SSM_INNER ** -0.5),
        'w_lru_out': nrm(29, (DEPTH, LRU_WIDTH, D_MODEL), LRU_WIDTH ** -0.5),
        'w_o': nrm(30, (DEPTH, D_MODEL, D_MODEL), D_MODEL ** -0.5),
        'ffn_w1': nrm(31, (N_DENSE, D_MODEL, D_FF), D_MODEL ** -0.5),
        'ffn_w3': nrm(32, (N_DENSE, D_MODEL, D_FF), D_MODEL ** -0.5),
        'ffn_w2': nrm(33, (N_DENSE, D_FF, D_MODEL), D_FF ** -0.5),
        'moe_router': nrm(34, (N_MOE, D_MODEL, N_EXPERTS), D_MODEL ** -0.5),
        'moe_w1': nrm(35, (N_MOE, N_EXPERTS, D_MODEL, D_FF), D_MODEL ** -0.5),
        'moe_w3': nrm(36, (N_MOE, N_EXPERTS, D_MODEL, D_FF), D_MODEL ** -0.5),
        'moe_w2': nrm(37, (N_MOE, N_EXPERTS, D_FF, D_MODEL), D_FF ** -0.5),
    }


def reference(x_prompt, x_sample, cache_k, cache_v, cache_kidx, state_ssm, state_ssm_conv, state_lru,
              state_lru_conv, norm_mix, norm_ffn, norm_final, w_in, ssm_conv_w, ssm_conv_b, ssm_dt_bias,
              ssm_a_log, ssm_d, ssm_norm, lru_conv_w, lru_conv_b, lru_wa, lru_ba, lru_wx, lru_bx, lru_lambda,
              w_att_out, w_ssm_out, w_lru_out, w_o, ffn_w1, ffn_w3, ffn_w2, moe_router, moe_w1, moe_w3, moe_w2):
    W = {'norm_mix': norm_mix, 'norm_ffn': norm_ffn, 'norm_final': norm_final, 'w_in': w_in,
         'ssm_conv_w': ssm_conv_w, 'ssm_conv_b': ssm_conv_b, 'ssm_dt_bias': ssm_dt_bias,
         'ssm_a_log': ssm_a_log, 'ssm_d': ssm_d, 'ssm_norm': ssm_norm,
         'lru_conv_w': lru_conv_w, 'lru_conv_b': lru_conv_b, 'lru_wa': lru_wa, 'lru_ba': lru_ba,
         'lru_wx': lru_wx, 'lru_bx': lru_bx, 'lru_lambda': lru_lambda,
         'w_att_out': w_att_out, 'w_ssm_out': w_ssm_out, 'w_lru_out': w_lru_out, 'w_o': w_o,
         'ffn_w1': ffn_w1, 'ffn_w3': ffn_w3, 'ffn_w2': ffn_w2,
         'moe_router': moe_router, 'moe_w1': moe_w1, 'moe_w3': moe_w3, 'moe_w2': moe_w2}
    seq = x_prompt.shape[1]
    topk_prompt = min(TOPK_MAX, seq // 4)
    y_prompt, p_states = run_trunk(x_prompt, jnp.arange(seq), None, W, topk_prompt)
    past_len = cache_k.shape[2]
    dec_seq = x_sample.shape[1]
    topk_sample = min(TOPK_MAX, (past_len + dec_seq) // 4)
    past = (cache_k, cache_v, cache_kidx, state_ssm, state_ssm_conv, state_lru, state_lru_conv)
    y_sample, s_states = run_trunk(x_sample, past_len + jnp.arange(dec_seq), past, W, topk_sample)
    p_k, p_v, p_kidx, p_ssm, p_ssm_conv, p_lru, p_lru_conv = p_states
    s_k, s_v, s_kidx, s_ssm, s_ssm_conv, s_lru, s_lru_conv = s_states
    return (y_prompt, y_sample, p_k, p_v, p_kidx, p_ssm, p_ssm_conv, p_lru, p_lru_conv,
            s_k, s_v, s_kidx, s_ssm, s_ssm_conv, s_lru, s_lru_conv)
```

```python
import functools

import jax
import jax.numpy as jnp
import numpy as np
from jax import lax
from jax.experimental import pallas as pl
from jax.experimental.pallas import tpu as pltpu

F32 = jnp.float32
BF16 = jnp.bfloat16
I32 = jnp.int32

CHUNK = 64
HEAD_DIM = 64
ATT_HEADS = 8
KV_HEADS = 2
IDX_HEADS = 4
IDX_DIM = 64
TOPK_MAX = 256
ROPE_THETA = 10000.0
MIX = 512
SSM_HEADS = 8
SSM_HEAD_DIM = 64
SSM_GROUPS = 2
SSM_STATE = 64
SSM_BC = 2 * SSM_GROUPS * SSM_STATE
LRU_C = 8.0
N_EXPERTS = 8
NORM_EPS = 1e-6

LANES = 128
VMEM_LIMIT = 56 * 1024 * 1024

COL_Q, COL_Z, COL_XL, COL_GL, COL_GATES = 0, 512, 1024, 1536, 2048
COL_XS, COL_DT, COL_BC, COL_QI, COL_K, COL_V, COL_KIWI = 5120, 5632, 6144, 6400, 6656, 6784, 6912
PROJ_W = 7168

NEG_BIG = -1e30
KEY_NEG_INF = -2139095041
INT_MAX = 2147483647


def _cparams(sem):
    return pltpu.CompilerParams(dimension_semantics=sem, vmem_limit_bytes=VMEM_LIMIT)


def _rms(x, g):
    ms = jnp.mean(x * x, axis=-1, keepdims=True)
    return x * lax.rsqrt(ms + NORM_EPS) * g


def _softplus(x):
    return jnp.maximum(x, 0.0) + jnp.log1p(jnp.exp(-jnp.abs(x)))


def _silu(x):
    return x * jax.nn.sigmoid(x)


def _norm_matmul_kernel(x_ref, g_ref, w_ref, o_ref, xn_ref):
    @pl.when(pl.program_id(1) == 0)
    def _():
        xn_ref[...] = _rms(x_ref[...], g_ref[...]).astype(BF16)

    o_ref[...] = jnp.dot(xn_ref[...], w_ref[...], preferred_element_type=F32)


def _norm_matmul(x, g, w, *, tm, tn):
    n, d = x.shape
    c = w.shape[1]
    return pl.pallas_call(
        _norm_matmul_kernel,
        grid=(n // tm, c // tn),
        in_specs=[pl.BlockSpec((tm, d), lambda i, j: (i, 0)),
                  pl.BlockSpec((1, d), lambda i, j: (0, 0)),
                  pl.BlockSpec((d, tn), lambda i, j: (0, j))],
        out_specs=pl.BlockSpec((tm, tn), lambda i, j: (i, j)),
        out_shape=jax.ShapeDtypeStruct((n, c), F32),
        scratch_shapes=[pltpu.VMEM((tm, d), BF16)],
        compiler_params=_cparams(("parallel", "arbitrary")),
        name="norm_in_proj",
    )(x, g, w)


def _rope_apply(x, cos, sin_signed, first_half):
    w = x.shape[1]
    reps = w // LANES
    if reps > 1:
        cos = jnp.concatenate([cos] * reps, axis=1)
        sin_signed = jnp.concatenate([sin_signed] * reps, axis=1)
        first_half = jnp.concatenate([first_half] * reps, axis=1)
    up = pltpu.roll(x, w - 32, axis=1)
    dn = pltpu.roll(x, 32, axis=1)
    return x * cos + jnp.where(first_half, up, dn) * sin_signed


def _rope_kernel(q_ref, qi_ref, k_ref, v_ref, kiwi_ref, cos_ref, sin_ref,
                 qo_ref, qio_ref, ko_ref, vo_ref, kio_ref, kbo_ref, vbo_ref, kibo_ref):
    cos = cos_ref[...]
    sin = sin_ref[...]
    lane = lax.broadcasted_iota(I32, cos.shape, 1)
    first_half = (lane % 64) < 32
    qo_ref[...] = _rope_apply(q_ref[...], cos, sin, first_half).astype(BF16)
    qio_ref[...] = _rope_apply(qi_ref[...], cos, sin, first_half).astype(BF16)
    k = _rope_apply(k_ref[...], cos, sin, first_half)
    ko_ref[...] = k
    kbo_ref[...] = k.astype(BF16)
    v = v_ref[...]
    vo_ref[...] = v
    vbo_ref[...] = v.astype(BF16)
    ki = _rope_apply(kiwi_ref[...], cos, sin, first_half)[:, :IDX_DIM]
    kio_ref[...] = ki
    kibo_ref[...] = ki.astype(BF16)


def _rope_call(proj, cos, sin, *, row0, nb, length, tl):
    nt = length // tl
    rb0 = row0 // tl
    n = nb * length

    def rows(b, j):
        return rb0 + b * nt + j

    def pspec(width, col):
        return pl.BlockSpec((tl, width), lambda b, j: (rows(b, j), col // width))

    def ospec(width):
        return pl.BlockSpec((tl, width), lambda b, j: (b * nt + j, 0))

    tspec = pl.BlockSpec((tl, LANES), lambda b, j: (j, 0))
    return pl.pallas_call(
        _rope_kernel,
        grid=(nb, nt),
        in_specs=[pspec(512, COL_Q), pspec(256, COL_QI), pspec(128, COL_K), pspec(128, COL_V),
                  pspec(128, COL_KIWI), tspec, tspec],
        out_specs=[ospec(512), ospec(256), ospec(128), ospec(128), ospec(IDX_DIM),
                   ospec(128), ospec(128), ospec(IDX_DIM)],
        out_shape=[jax.ShapeDtypeStruct((n, 512), BF16), jax.ShapeDtypeStruct((n, 256), BF16),
                   jax.ShapeDtypeStruct((n, 128), F32), jax.ShapeDtypeStruct((n, 128), F32),
                   jax.ShapeDtypeStruct((n, IDX_DIM), F32),
                   jax.ShapeDtypeStruct((n, 128), BF16), jax.ShapeDtypeStruct((n, 128), BF16),
                   jax.ShapeDtypeStruct((n, IDX_DIM), BF16)],
        compiler_params=_cparams(("parallel", "parallel")),
        name="rope",
    )(proj, proj, proj, proj, proj, cos, sin)


def _attn_kernel(q_ref, qi_ref, kiwi_ref, k_ref, v_ref, ki_ref, o_ref, key_ref, bias_ref,
                 *, tq, kb, s_valid, q_off, topk):
    i = pl.program_id(1)
    t0 = i * tq
    q_last = q_off + t0 + tq - 1
    n_adm = jnp.minimum((q_last // CHUNK + 1) * CHUNK, s_valid)
    nkb = (n_adm + kb - 1) // kb
    nsub = kb // LANES

    wi = kiwi_ref[:, IDX_DIM:IDX_DIM + IDX_HEADS]
    q_chunk = (q_off + t0 + lax.broadcasted_iota(I32, (tq, 1), 0)) // CHUNK
    nt_dims = (((1,), (1,)), ((), ()))

    def score_body(j, carry):
        off = pl.multiple_of(j * kb, kb)
        ki_blk = ki_ref[pl.ds(off, kb), :]
        sc = jnp.zeros((tq, kb), F32)
        for h in range(IDX_HEADS):
            qh = qi_ref[:, h * IDX_DIM:(h + 1) * IDX_DIM]
            s = lax.dot_general(qh, ki_blk, nt_dims, preferred_element_type=F32)
            sc = sc + jnp.maximum(s, 0.0) * wi[:, h:h + 1]
        kpos = off + lax.broadcasted_iota(I32, (1, kb), 1)
        adm = ((kpos // CHUNK) <= q_chunk) & (kpos < s_valid)
        sc = jnp.where(sc == 0.0, 0.0, sc)
        sc = jnp.where(adm, sc, -jnp.inf)
        bits = pltpu.bitcast(sc, I32)
        key_ref[j] = bits ^ ((bits >> 31) & INT_MAX)
        return carry

    lax.fori_loop(0, nkb, score_body, 0)

    def count_ge(thr):
        thr_b = jnp.broadcast_to(thr, (tq, LANES))

        def body(j, acc):
            blk = key_ref[j]
            for c in range(nsub):
                acc = acc + jnp.where(blk[:, c * LANES:(c + 1) * LANES] >= thr_b, 1, 0)
            return acc

        acc = lax.fori_loop(0, nkb, body, jnp.zeros((tq, LANES), I32))
        return jnp.sum(acc, axis=1, keepdims=True)

    def bis_body(_, lohi):
        lo, hi = lohi
        mid = (lo >> 1) + (hi >> 1) + (lo & hi & 1)
        ge = count_ge(mid) >= topk
        return jnp.where(ge, mid, lo), jnp.where(ge, hi, mid)

    lo0 = jnp.full((tq, 1), KEY_NEG_INF, I32)
    hi0 = jnp.full((tq, 1), INT_MAX, I32)
    thr, _ = lax.fori_loop(0, 32, bis_body, (lo0, hi0))
    cnt_ge = count_ge(thr)
    has_ties = jnp.max(cnt_ge) > topk

    @pl.when(jnp.logical_not(has_ties))
    def _():
        thr_eff = jnp.maximum(thr, KEY_NEG_INF + 1)

        def body(j, carry):
            bias_ref[j] = jnp.where(key_ref[j] >= thr_eff, 0.0, NEG_BIG)
            return carry

        lax.fori_loop(0, nkb, body, 0)

    @pl.when(has_ties)
    def _():
        cnt_gt = count_ge(thr + 1)
        need = (topk - cnt_gt).astype(F32)
        thr_ok = thr > KEY_NEG_INF
        r = lax.broadcasted_iota(I32, (kb, kb), 0)
        c = lax.broadcasted_iota(I32, (kb, kb), 1)
        upper = jnp.where(r < c, 1.0, 0.0).astype(BF16)

        def body(j, carry):
            key = key_ref[j]
            gt = key > thr
            eq = (key == thr) & thr_ok
            eqf = jnp.where(eq, 1.0, 0.0)
            rank = jnp.dot(eqf.astype(BF16), upper, preferred_element_type=F32) + carry
            sel = gt | (eq & (rank < need))
            bias_ref[j] = jnp.where(sel, 0.0, NEG_BIG)
            return carry + jnp.sum(eqf, axis=1, keepdims=True)

        lax.fori_loop(0, nkb, body, jnp.zeros((tq, 1), F32))

    for h in range(ATT_HEADS):
        g = h // (ATT_HEADS // KV_HEADS)
        qh = q_ref[:, h * HEAD_DIM:(h + 1) * HEAD_DIM]

        def att_body(j, mla, qh=qh, g=g):
            m, l, acc = mla
            off = pl.multiple_of(j * kb, kb)
            kblk = k_ref[pl.ds(off, kb), g * HEAD_DIM:(g + 1) * HEAD_DIM]
            vblk = v_ref[pl.ds(off, kb), g * HEAD_DIM:(g + 1) * HEAD_DIM]
            s = lax.dot_general(qh, kblk, nt_dims, preferred_element_type=F32) + bias_ref[j]
            m_new = jnp.maximum(m, jnp.max(s, axis=1, keepdims=True))
            alpha = jnp.exp(m - m_new)
            p = jnp.exp(s - m_new)
            l = alpha * l + jnp.sum(p, axis=1, keepdims=True)
            acc = alpha * acc + jnp.dot(p.astype(BF16), vblk, preferred_element_type=F32)
            return m_new, l, acc

        init = (jnp.full((tq, 1), NEG_BIG, F32), jnp.zeros((tq, 1), F32),
                jnp.zeros((tq, HEAD_DIM), F32))
        _, l, acc = lax.fori_loop(0, nkb, att_body, init)
        o_ref[:, h * HEAD_DIM:(h + 1) * HEAD_DIM] = (acc / l).astype(BF16)


def _attn_call(q, qi, proj, k, v, ki, *, row0, nb, length, s_pad, s_valid, q_off, tq, kb, topk):
    nq = length // tq
    rb0 = row0 // tq
    nkb_max = s_pad // kb
    kern = functools.partial(_attn_kernel, tq=tq, kb=kb, s_valid=s_valid, q_off=q_off, topk=topk)
    return pl.pallas_call(
        kern,
        grid=(nb, nq),
        in_specs=[pl.BlockSpec((tq, 512), lambda b, i: (b * nq + i, 0)),
                  pl.BlockSpec((tq, 256), lambda b, i: (b * nq + i, 0)),
                  pl.BlockSpec((tq, 128), lambda b, i: (rb0 + b * nq + i, COL_KIWI // 128)),
                  pl.BlockSpec((s_pad, 128), lambda b, i: (b, 0)),
                  pl.BlockSpec((s_pad, 128), lambda b, i: (b, 0)),
                  pl.BlockSpec((s_pad, IDX_DIM), lambda b, i: (b, 0))],
        out_specs=pl.BlockSpec((tq, 512), lambda b, i: (b * nq + i, 0)),
        out_shape=jax.ShapeDtypeStruct((nb * length, 512), BF16),
        scratch_shapes=[pltpu.VMEM((nkb_max, tq, kb), I32), pltpu.VMEM((nkb_max, tq, kb), F32)],
        compiler_params=_cparams(("parallel", "arbitrary")),
        name="dsa_attention",
    )(q, qi, proj, k, v, ki)


def _ssm_kernel(z_ref, xs_ref, dt_ref, bc_ref, cx0_ref, cbc0_ref, s0_ref,
                cwx_ref, cwbc_ref, cbx_ref, cbbc_ref, dtb_ref, alog_ref, dsk_ref, nw_ref,
                y_ref, sT_ref, cxT_ref, cbcT_ref,
                xpx_ref, xpbc_ref, xc_ref, bcc_ref, dtc_ref, ypre_ref, st_ref,
                *, t_in, t_pad):
    j = pl.program_id(1)
    nj = pl.num_programs(1)
    q = CHUNK

    @pl.when(j == 0)
    def _():
        xpx_ref[0:8, :] = cx0_ref[...]
        xpbc_ref[0:8, :] = cbc0_ref[...]
        st_ref[...] = s0_ref[...]

    xpx_ref[8:8 + t_in, :] = xs_ref[...]
    xpbc_ref[8:8 + t_in, :] = bc_ref[...]

    def conv(xp_ref, w_ref, b_ref):
        y = b_ref[...]
        for tap in range(4):
            y = y + xp_ref[5 + tap:5 + tap + t_in, :] * w_ref[tap:tap + 1, :]
        return _silu(y)

    if t_pad > t_in:
        xc_ref[...] = jnp.zeros_like(xc_ref)
        bcc_ref[...] = jnp.zeros_like(bcc_ref)
        dtc_ref[...] = jnp.zeros_like(dtc_ref)
    xc_ref[0:t_in, :] = conv(xpx_ref, cwx_ref, cbx_ref)
    bcc_ref[0:t_in, :] = conv(xpbc_ref, cwbc_ref, cbbc_ref)
    dtc_ref[0:t_in, :] = _softplus(dt_ref[...] + dtb_ref[...])

    last_x = xpx_ref[t_in:t_in + 8, :]
    last_bc = xpbc_ref[t_in:t_in + 8, :]
    xpx_ref[0:8, :] = last_x
    xpbc_ref[0:8, :] = last_bc

    a_neg = -jnp.exp(alog_ref[...])
    li = lax.broadcasted_iota(I32, (q, q), 0)
    si = lax.broadcasted_iota(I32, (q, q), 1)
    tri = jnp.where(si <= li, 1.0, 0.0)
    ones = jnp.ones((q, q), F32)
    lane = lax.broadcasted_iota(I32, (q, MIX), 1)
    row = lax.broadcasted_iota(I32, (q, MIX), 0)
    s_of_lane = lane % q
    mask_t_le_s = jnp.where(row <= s_of_lane, 1.0, 0.0)
    causal = s_of_lane <= row
    rg = lax.broadcasted_iota(I32, (SSM_HEADS * q, 2 * SSM_STATE), 0) // (q * SSM_HEADS // SSM_GROUPS)
    cg = lax.broadcasted_iota(I32, (SSM_HEADS * q, 2 * SSM_STATE), 1) // SSM_STATE
    gmask = rg == cg
    rh = lax.broadcasted_iota(I32, (SSM_HEADS * q, MIX), 0) // q
    ch = lax.broadcasted_iota(I32, (SSM_HEADS * q, MIX), 1) // SSM_HEAD_DIM
    hmask = rh == ch
    r2 = lax.broadcasted_iota(I32, (2 * SSM_STATE, MIX), 0) // SSM_STATE
    c2 = lax.broadcasted_iota(I32, (2 * SSM_STATE, MIX), 1) // (MIX // SSM_GROUPS)
    g2mask = r2 == c2
    hp = lax.Precision.HIGHEST
    nt_dims = (((1,), (1,)), ((), ()))
    tn_dims = (((0,), (0,)), ((), ()))

    def chunk_body(c, carry):
        r0 = pl.multiple_of(c * q, q)
        xs = xc_ref[pl.ds(r0, q), :]
        dt = dtc_ref[pl.ds(r0, q), :]
        bmat = bcc_ref[pl.ds(r0, q), 0:2 * SSM_STATE]
        cmat = bcc_ref[pl.ds(r0, q), 2 * SSM_STATE:4 * SSM_STATE]
        a = dt * a_neg
        xdt = xs * dt
        acum = jnp.dot(tri, a, precision=hp, preferred_element_type=F32)
        rowt = jnp.dot(ones, a * mask_t_le_s, precision=hp, preferred_element_type=F32)
        decay_in = jnp.where(causal, jnp.exp(acum - rowt), 0.0)
        bexp = jnp.where(gmask, jnp.concatenate([bmat] * SSM_HEADS, axis=0), 0.0)
        cb = lax.dot_general(cmat.astype(BF16), bexp.astype(BF16), nt_dims,
                             preferred_element_type=F32)
        m = (cb * decay_in).astype(BF16)
        bdx = jnp.where(hmask, jnp.concatenate([xdt] * SSM_HEADS, axis=0), 0.0).astype(BF16)
        y_diag = jnp.dot(m, bdx, preferred_element_type=F32)
        st = st_ref[...]
        y_off = jnp.exp(acum) * jnp.dot(cmat.astype(BF16), st.astype(BF16),
                                        preferred_element_type=F32)
        a_end = acum[q - 1:q, :]
        xd = (xdt * jnp.exp(a_end - acum)).astype(BF16)
        upd = lax.dot_general(bmat.astype(BF16), xd, tn_dims, preferred_element_type=F32)
        st_ref[...] = jnp.exp(a_end) * st + jnp.where(g2mask, upd, 0.0)
        ypre_ref[pl.ds(r0, q), :] = y_diag + y_off
        return carry

    lax.fori_loop(0, t_pad // q, chunk_body, 0)

    xs = xc_ref[0:t_in, :]
    y = ypre_ref[0:t_in, :] + dsk_ref[...] * xs
    y = y * _silu(z_ref[...])
    half = MIX // SSM_GROUPS
    parts = []
    for g in range(SSM_GROUPS):
        yg = y[:, g * half:(g + 1) * half]
        parts.append(yg * lax.rsqrt(jnp.mean(yg * yg, axis=-1, keepdims=True) + NORM_EPS))
    y = jnp.concatenate(parts, axis=1) * nw_ref[...]
    y_ref[...] = y.astype(BF16)

    @pl.when(j == nj - 1)
    def _():
        sT_ref[...] = st_ref[...]
        cxT_ref[...] = last_x
        cbcT_ref[...] = last_bc


def _ssm_call(proj, cx0, cbc0, s0, wts, *, row0, nb, length, tl):
    nt = length // tl
    rb0 = row0 // tl
    t_pad = -(-tl // CHUNK) * CHUNK
    kern = functools.partial(_ssm_kernel, t_in=tl, t_pad=t_pad)

    def pspec(width, col):
        return pl.BlockSpec((tl, width), lambda b, j: (rb0 + b * nt + j, col // width))

    def bspec(r, c):
        return pl.BlockSpec((None, r, c), lambda b, j: (b, 0, 0))

    def wspec(r, c):
        return pl.BlockSpec((r, c), lambda b, j: (0, 0))

    return pl.pallas_call(
        kern,
        grid=(nb, nt),
        in_specs=[pspec(512, COL_Z), pspec(512, COL_XS), pspec(512, COL_DT), pspec(256, COL_BC),
                  bspec(8, MIX), bspec(8, SSM_BC), bspec(2 * SSM_STATE, MIX),
                  wspec(4, MIX), wspec(4, SSM_BC), wspec(1, MIX), wspec(1, SSM_BC),
                  wspec(1, MIX), wspec(1, MIX), wspec(1, MIX), wspec(1, MIX)],
        out_specs=[pl.BlockSpec((tl, MIX), lambda b, j: (b * nt + j, 0)),
                   bspec(2 * SSM_STATE, MIX), bspec(8, MIX), bspec(8, SSM_BC)],
        out_shape=[jax.ShapeDtypeStruct((nb * length, MIX), BF16),
                   jax.ShapeDtypeStruct((nb, 2 * SSM_STATE, MIX), F32),
                   jax.ShapeDtypeStruct((nb, 8, MIX), F32),
                   jax.ShapeDtypeStruct((nb, 8, SSM_BC), F32)],
        scratch_shapes=[pltpu.VMEM((tl + 8, MIX), F32), pltpu.VMEM((tl + 8, SSM_BC), F32),
                        pltpu.VMEM((t_pad, MIX), F32), pltpu.VMEM((t_pad, SSM_BC), F32),
                        pltpu.VMEM((t_pad, MIX), F32), pltpu.VMEM((t_pad, MIX), F32),
                        pltpu.VMEM((2 * SSM_STATE, MIX), F32)],
        compiler_params=_cparams(("parallel", "arbitrary")),
        name="ssd_branch",
    )(proj, proj, proj, proj, cx0, cbc0, s0, *wts)


def _lru_kernel(xl_ref, gl_ref, c0_ref, h0_ref, cw_ref, cb_ref, wa_ref, ba_ref, wx_ref, bx_ref,
                lam_ref, y_ref, hT_ref, cT_ref, xp_ref, a_ref, u_ref, hs_ref, h_ref, *, tl):
    j = pl.program_id(1)
    nj = pl.num_programs(1)

    @pl.when(j == 0)
    def _():
        xp_ref[0:8, :] = c0_ref[...]
        h_ref[...] = h0_ref[...]

    xp_ref[8:8 + tl, :] = xl_ref[...]
    xc = cb_ref[...]
    for tap in range(4):
        xc = xc + xp_ref[5 + tap:5 + tap + tl, :] * cw_ref[tap:tap + 1, :]
    last = xp_ref[tl:tl + 8, :]
    xp_ref[0:8, :] = last

    xcb = xc.astype(BF16)
    r = jax.nn.sigmoid(jnp.dot(xcb, wa_ref[...], preferred_element_type=F32) + ba_ref[...])
    i = jax.nn.sigmoid(jnp.dot(xcb, wx_ref[...], preferred_element_type=F32) + bx_ref[...])
    log_a = (-LRU_C * _softplus(-lam_ref[...])) * r
    a = jnp.exp(log_a)
    mult = jnp.sqrt(-jnp.tanh(log_a) * (a * a + 1.0))
    a_ref[...] = a
    u_ref[...] = mult * (i * xc)

    def step(t, h):
        h = a_ref[pl.ds(t, 1), :] * h + u_ref[pl.ds(t, 1), :]
        hs_ref[pl.ds(t, 1), :] = h
        return h

    h = lax.fori_loop(0, tl, step, h_ref[...], unroll=8)
    h_ref[...] = h
    y_ref[...] = (hs_ref[...] * jax.nn.gelu(gl_ref[...], approximate=True)).astype(BF16)

    @pl.when(j == nj - 1)
    def _():
        hT_ref[...] = h
        cT_ref[...] = last


def _lru_call(proj, c0, h0, wts, *, row0, nb, length, tl):
    nt = length // tl
    rb0 = row0 // tl
    kern = functools.partial(_lru_kernel, tl=tl)

    def pspec(col):
        return pl.BlockSpec((tl, MIX), lambda b, j: (rb0 + b * nt + j, col // MIX))

    def bspec(r):
        return pl.BlockSpec((None, r, MIX), lambda b, j: (b, 0, 0))

    def wspec(r):
        return pl.BlockSpec((r, MIX), lambda b, j: (0, 0))

    return pl.pallas_call(
        kern,
        grid=(nb, nt),
        in_specs=[pspec(COL_XL), pspec(COL_GL), bspec(8), bspec(1),
                  wspec(4), wspec(1), wspec(MIX), wspec(1), wspec(MIX), wspec(1), wspec(1)],
        out_specs=[pl.BlockSpec((tl, MIX), lambda b, j: (b * nt + j, 0)), bspec(1), bspec(8)],
        out_shape=[jax.ShapeDtypeStruct((nb * length, MIX), BF16),
                   jax.ShapeDtypeStruct((nb, 1, MIX), F32),
                   jax.ShapeDtypeStruct((nb, 8, MIX), F32)],
        scratch_shapes=[pltpu.VMEM((tl + 8, MIX), F32), pltpu.VMEM((tl, MIX), F32),
                        pltpu.VMEM((tl, MIX), F32), pltpu.VMEM((tl, MIX), F32),
                        pltpu.VMEM((1, MIX), F32)],
        compiler_params=_cparams(("parallel", "arbitrary")),
        name="rglru_branch",
    )(proj, proj, c0, h0, *wts)


def _merge_kernel(x_ref, att_ref, ssm_ref, lru_ref, g0_ref, g1_ref, g2_ref,
                  wa_ref, ws_ref, wl_ref, wo_ref, o_ref):
    merged = (jax.nn.sigmoid(g0_ref[...]) * jnp.dot(att_ref[...], wa_ref[...], preferred_element_type=F32)
              + jax.nn.sigmoid(g1_ref[...]) * jnp.dot(ssm_ref[...], ws_ref[...], preferred_element_type=F32)
              + jax.nn.sigmoid(g2_ref[...]) * jnp.dot(lru_ref[...], wl_ref[...], preferred_element_type=F32))
    o_ref[...] = x_ref[...] + jnp.dot(merged.astype(BF16), wo_ref[...], preferred_element_type=F32)


def _merge_call(x, att, ssm, lru, proj, wa, ws, wl, wo, *, tm):
    n, d = x.shape

    def rspec(width):
        return pl.BlockSpec((tm, width), lambda i: (i, 0))

    def gspec(k):
        return pl.BlockSpec((tm, d), lambda i: (i, COL_GATES // d + k))

    def wspec(r):
        return pl.BlockSpec((r, d), lambda i: (0, 0))

    return pl.pallas_call(
        _merge_kernel,
        grid=(n // tm,),
        in_specs=[rspec(d), rspec(MIX), rspec(MIX), rspec(MIX), gspec(0), gspec(1), gspec(2),
                  wspec(MIX), wspec(MIX), wspec(MIX), wspec(d)],
        out_specs=rspec(d),
        out_shape=jax.ShapeDtypeStruct((n, d), F32),
        compiler_params=_cparams(("parallel",)),
        name="branch_merge",
    )(x, att, ssm, lru, proj, proj, proj, wa, ws, wl, wo)


def _ffn_kernel(x_ref, g_ref, w1_ref, w3_ref, w2_ref, o_ref, xn_ref, acc_ref):
    f = pl.program_id(1)

    @pl.when(f == 0)
    def _():
        xn_ref[...] = _rms(x_ref[...], g_ref[...]).astype(BF16)
        acc_ref[...] = jnp.zeros_like(acc_ref)

    xn = xn_ref[...]
    h1 = jnp.dot(xn, w1_ref[...], preferred_element_type=F32)
    h3 = jnp.dot(xn, w3_ref[...], preferred_element_type=F32)
    h = (_silu(h1) * h3).astype(BF16)
    acc_ref[...] += jnp.dot(h, w2_ref[...], preferred_element_type=F32)

    @pl.when(f == pl.num_programs(1) - 1)
    def _():
        o_ref[...] = x_ref[...] + acc_ref[...]


def _ffn_call(x, g, w1, w3, w2, *, tm, tf):
    n, d = x.shape
    dff = w1.shape[1]
    return pl.pallas_call(
        _ffn_kernel,
        grid=(n // tm, dff // tf),
        in_specs=[pl.BlockSpec((tm, d), lambda i, f: (i, 0)),
                  pl.BlockSpec((1, d), lambda i, f: (0, 0)),
                  pl.BlockSpec((d, tf), lambda i, f: (0, f)),
                  pl.BlockSpec((d, tf), lambda i, f: (0, f)),
                  pl.BlockSpec((tf, d), lambda i, f: (f, 0))],
        out_specs=pl.BlockSpec((tm, d), lambda i, f: (i, 0)),
        out_shape=jax.ShapeDtypeStruct((n, d), F32),
        scratch_shapes=[pltpu.VMEM((tm, d), BF16), pltpu.VMEM((tm, d), F32)],
        compiler_params=_cparams(("parallel", "arbitrary")),
        name="swiglu_ffn",
    )(x, g, w1, w3, w2)


def _router_kernel(x_ref, g_ref, wr_ref, gate_ref):
    xn = _rms(x_ref[...], g_ref[...])
    logits = jnp.dot(xn, wr_ref[...], precision=lax.Precision.HIGHEST, preferred_element_type=F32)
    lane = lax.broadcasted_iota(I32, logits.shape, 1)
    logits = jnp.where(lane < N_EXPERTS, logits, -jnp.inf)
    m1 = jnp.max(logits, axis=1, keepdims=True)
    i1 = jnp.min(jnp.where(logits == m1, lane, LANES), axis=1, keepdims=True)
    rest = jnp.where(lane == i1, -jnp.inf, logits)
    m2 = jnp.max(rest, axis=1, keepdims=True)
    i2 = jnp.min(jnp.where(rest == m2, lane, LANES), axis=1, keepdims=True)
    e2 = jnp.exp(m2 - m1)
    den = 1.0 + e2
    gate_ref[...] = jnp.where(lane == i1, 1.0 / den, 0.0) + jnp.where(lane == i2, e2 / den, 0.0)


def _router_call(x, g, wr, *, tm):
    n, d = x.shape
    return pl.pallas_call(
        _router_kernel,
        grid=(n // tm,),
        in_specs=[pl.BlockSpec((tm, d), lambda i: (i, 0)),
                  pl.BlockSpec((1, d), lambda i: (0, 0)),
                  pl.BlockSpec((d, LANES), lambda i: (0, 0))],
        out_specs=pl.BlockSpec((tm, LANES), lambda i: (i, 0)),
        out_shape=jax.ShapeDtypeStruct((n, LANES), F32),
        compiler_params=_cparams(("parallel",)),
        name="moe_router",
    )(x, g, wr)


def _moe_kernel(x_ref, g_ref, gate_ref, w1_ref, w3_ref, w2_ref, o_ref, xn_ref, acc_ref):
    e = pl.program_id(1)
    f = pl.program_id(2)

    @pl.when((e == 0) & (f == 0))
    def _():
        xn_ref[...] = _rms(x_ref[...], g_ref[...]).astype(BF16)
        acc_ref[...] = jnp.zeros_like(acc_ref)

    gate = gate_ref[...]
    lane = lax.broadcasted_iota(I32, gate.shape, 1)
    ge = jnp.sum(jnp.where(lane == e, gate, 0.0), axis=1, keepdims=True)
    xn = xn_ref[...]
    h1 = jnp.dot(xn, w1_ref[...], preferred_element_type=F32)
    h3 = jnp.dot(xn, w3_ref[...], preferred_element_type=F32)
    h = (_silu(h1) * h3 * ge).astype(BF16)
    acc_ref[...] += jnp.dot(h, w2_ref[...], preferred_element_type=F32)

    @pl.when((e == pl.num_programs(1) - 1) & (f == pl.num_programs(2) - 1))
    def _():
        o_ref[...] = x_ref[...] + acc_ref[...]


def _moe_call(x, g, gate, w1, w3, w2, *, tm, tf):
    n, d = x.shape
    ne, _, dff = w1.shape
    return pl.pallas_call(
        _moe_kernel,
        grid=(n // tm, ne, dff // tf),
        in_specs=[pl.BlockSpec((tm, d), lambda i, e, f: (i, 0)),
                  pl.BlockSpec((1, d), lambda i, e, f: (0, 0)),
                  pl.BlockSpec((tm, LANES), lambda i, e, f: (i, 0)),
                  pl.BlockSpec((None, d, tf), lambda i, e, f: (e, 0, f)),
                  pl.BlockSpec((None, d, tf), lambda i, e, f: (e, 0, f)),
                  pl.BlockSpec((None, tf, d), lambda i, e, f: (e, f, 0))],
        out_specs=pl.BlockSpec((tm, d), lambda i, e, f: (i, 0)),
        out_shape=jax.ShapeDtypeStruct((n, d), F32),
        scratch_shapes=[pltpu.VMEM((tm, d), BF16), pltpu.VMEM((tm, d), F32)],
        compiler_params=_cparams(("parallel", "arbitrary", "arbitrary")),
        name="moe_swiglu",
    )(x, g, gate, w1, w3, w2)


def _norm_kernel(x_ref, g_ref, o_ref):
    o_ref[...] = _rms(x_ref[...], g_ref[...])


def _norm_call(x, g, *, tm):
    n, d = x.shape
    return pl.pallas_call(
        _norm_kernel,
        grid=(n // tm,),
        in_specs=[pl.BlockSpec((tm, d), lambda i: (i, 0)), pl.BlockSpec((1, d), lambda i: (0, 0))],
        out_specs=pl.BlockSpec((tm, d), lambda i: (i, 0)),
        out_shape=jax.ShapeDtypeStruct((n, d), F32),
        compiler_params=_cparams(("parallel",)),
        name="final_norm",
    )(x, g)


def _pack_w_in(w):
    d = w.shape[0]
    o = np.cumsum([0, 512, 128, 128, 256, 64, 4, 512, 768, 8, 512, 512, 3072])
    seg = lambda k: w[:, int(o[k]):int(o[k + 1])]
    q, k, v, qi, ki, wi, z, xbc, dt, xl, gl, gates = [seg(t) for t in range(12)]
    xs, bc = xbc[:, :MIX], xbc[:, MIX:]
    dt_exp = jnp.repeat(dt, SSM_HEAD_DIM, axis=1)
    kiwi = jnp.concatenate([ki, wi, jnp.zeros((d, LANES - IDX_DIM - IDX_HEADS), w.dtype)], axis=1)
    packed = jnp.concatenate(
        [q * (HEAD_DIM ** -0.5), z, xl, gl, gates, xs, dt_exp, bc, qi, k, v, kiwi,
         jnp.zeros((d, PROJ_W - COL_KIWI - LANES), w.dtype)], axis=1)
    return packed.astype(BF16)


def _block_diag(w):
    nblk, bw, _ = w.shape
    eye = jnp.eye(nblk, dtype=w.dtype)
    return jnp.einsum('kij,kl->kilj', w, eye).reshape(nblk * bw, nblk * bw)


def _rope_tables(pos):
    half = HEAD_DIM // 2
    inv = 1.0 / (ROPE_THETA ** (jnp.arange(half, dtype=F32) / half))
    ang = pos.astype(F32)[:, None] * inv[None, :]
    cos, sin = jnp.cos(ang), jnp.sin(ang)
    cos_t = jnp.concatenate([cos, cos, cos, cos], axis=1)
    sin_t = jnp.concatenate([-sin, sin, -sin, sin], axis=1)
    return cos_t, sin_t


def _pad_rows8(a):
    return jnp.pad(a, ((0, 0), (5, 0), (0, 0)))


def _state_to_s2(h):
    nb = h.shape[0]
    hg = h.reshape(nb, SSM_GROUPS, SSM_HEADS // SSM_GROUPS, SSM_HEAD_DIM, SSM_STATE)
    eye = jnp.eye(SSM_GROUPS, dtype=h.dtype)
    s2 = jnp.einsum('bgkpn,gf->bfngkp', hg, eye)
    return s2.reshape(nb, SSM_GROUPS * SSM_STATE, MIX)


def _s2_to_state(s2):
    nb = s2.shape[0]
    s6 = s2.reshape(nb, SSM_GROUPS, SSM_STATE, SSM_GROUPS, SSM_HEADS // SSM_GROUPS, SSM_HEAD_DIM)
    diag = jnp.stack([s6[:, g, :, g] for g in range(SSM_GROUPS)], axis=1)
    return diag.transpose(0, 1, 3, 4, 2).reshape(nb, SSM_HEADS, SSM_HEAD_DIM, SSM_STATE)


def _expand_heads(v):
    return jnp.repeat(v, SSM_HEAD_DIM)[None, :]


def kernel(x_prompt, x_sample, cache_k, cache_v, cache_kidx, state_ssm, state_ssm_conv, state_lru,
           state_lru_conv, norm_mix, norm_ffn, norm_final, w_in, ssm_conv_w, ssm_conv_b, ssm_dt_bias,
           ssm_a_log, ssm_d, ssm_norm, lru_conv_w, lru_conv_b, lru_wa, lru_ba, lru_wx, lru_bx, lru_lambda,
           w_att_out, w_ssm_out, w_lru_out, w_o, ffn_w1, ffn_w3, ffn_w2, moe_router, moe_w1, moe_w3, moe_w2):
    pb, pl_len, d = x_prompt.shape
    sb, sl_len, _ = x_sample.shape
    depth = w_in.shape[0]
    past = cache_k.shape[2]
    n_p = pb * pl_len
    n_s = sb * sl_len
    x = jnp.concatenate([x_prompt.reshape(n_p, d), x_sample.reshape(n_s, d)], axis=0)

    topk_p = min(TOPK_MAX, pl_len // 4)
    s_tot = past + sl_len
    topk_s = min(TOPK_MAX, s_tot // 4)
    kb_s = LANES
    s_pad = -(-s_tot // kb_s) * kb_s

    cos_p, sin_p = _rope_tables(jnp.arange(pl_len))
    cos_s, sin_s = _rope_tables(past + jnp.arange(sl_len))

    groups = (
        dict(row0=0, nb=pb, length=pl_len),
        dict(row0=n_p, nb=sb, length=sl_len),
    )
    tl_p, tl_s = 256, sl_len

    collected = [[[] for _ in range(7)] for _ in range(2)]
    for layer in range(depth):
        proj = _norm_matmul(x, norm_mix[layer][None, :], _pack_w_in(w_in[layer]), tm=512, tn=512)

        ssm_w = (ssm_conv_w[layer][:, :MIX], ssm_conv_w[layer][:, MIX:],
                 ssm_conv_b[layer][None, :MIX], ssm_conv_b[layer][None, MIX:],
                 _expand_heads(ssm_dt_bias[layer]), _expand_heads(ssm_a_log[layer]),
                 _expand_heads(ssm_d[layer]), ssm_norm[layer][None, :])
        lru_w = (lru_conv_w[layer], lru_conv_b[layer][None, :],
                 _block_diag(lru_wa[layer]).astype(BF16), lru_ba[layer][None, :],
                 _block_diag(lru_wx[layer]).astype(BF16), lru_bx[layer][None, :],
                 lru_lambda[layer][None, :])

        branch = [[], [], []]
        for gi, grp in enumerate(groups):
            nb, length = grp['nb'], grp['length']
            if gi == 0:
                tl, cos, sin = tl_p, cos_p, sin_p
                cx0 = jnp.zeros((nb, 8, MIX), F32)
                cbc0 = jnp.zeros((nb, 8, SSM_BC), F32)
                s0 = jnp.zeros((nb, 2 * SSM_STATE, MIX), F32)
                lc0 = jnp.zeros((nb, 8, MIX), F32)
                lh0 = jnp.zeros((nb, 1, MIX), F32)
            else:
                tl, cos, sin = tl_s, cos_s, sin_s
                conv0 = _pad_rows8(state_ssm_conv[layer])
                cx0, cbc0 = conv0[:, :, :MIX], conv0[:, :, MIX:]
                s0 = _state_to_s2(state_ssm[layer])
                lc0 = _pad_rows8(state_lru_conv[layer])
                lh0 = state_lru[layer][:, None, :]

            q_r, qi_r, k_r, v_r, ki_r, k_b, v_b, ki_b = _rope_call(proj, cos, sin, tl=tl, **grp)
            if gi == 0:
                att = _attn_call(q_r, qi_r, proj, k_b, v_b, ki_b, s_pad=length, s_valid=length,
                                 q_off=0, tq=128, kb=256, topk=topk_p, **grp)
            else:
                def cat(cache, new, width):
                    c = cache.reshape(nb, past, width).astype(BF16)
                    a = jnp.concatenate([c, new.reshape(nb, length, width)], axis=1)
                    a = jnp.pad(a, ((0, 0), (0, s_pad - s_tot), (0, 0)))
                    return a.reshape(nb * s_pad, width)
                att = _attn_call(q_r, qi_r, proj, cat(cache_k[layer], k_b, 128), cat(cache_v[layer], v_b, 128),
                                 cat(cache_kidx[layer], ki_b, IDX_DIM), s_pad=s_pad, s_valid=s_tot,
                                 q_off=past, tq=length, kb=kb_s, topk=topk_s, **grp)
            y_ssm, s_t, cx_t, cbc_t = _ssm_call(proj, cx0, cbc0, s0, ssm_w, tl=tl, **grp)
            y_lru, lh_t, lc_t = _lru_call(proj, lc0, lh0, lru_w, tl=tl, **grp)
            branch[0].append(att)
            branch[1].append(y_ssm)
            branch[2].append(y_lru)

            st = (k_r.reshape(nb, length, KV_HEADS, HEAD_DIM), v_r.reshape(nb, length, KV_HEADS, HEAD_DIM),
                  ki_r.reshape(nb, length, IDX_DIM), _s2_to_state(s_t),
                  jnp.concatenate([cx_t[:, 5:], cbc_t[:, 5:]], axis=2), lh_t[:, 0], lc_t[:, 5:])
            for lst, s in zip(collected[gi], st):
                lst.append(s)

        att, y_ssm, y_lru = [jnp.concatenate(bl, axis=0) for bl in branch]
        x = _merge_call(x, att, y_ssm, y_lru, proj, w_att_out[layer].astype(BF16),
                        w_ssm_out[layer].astype(BF16), w_lru_out[layer].astype(BF16),
                        w_o[layer].astype(BF16), tm=256)
        jl = layer // 2
        gf = norm_ffn[layer][None, :]
        if layer % 2 == 0:
            x = _ffn_call(x, gf, ffn_w1[jl].astype(BF16), ffn_w3[jl].astype(BF16),
                          ffn_w2[jl].astype(BF16), tm=768, tf=256)
        else:
            wr = jnp.pad(moe_router[jl], ((0, 0), (0, LANES - N_EXPERTS)))
            gate = _router_call(x, gf, wr, tm=512)
            x = _moe_call(x, gf, gate, moe_w1[jl].astype(BF16), moe_w3[jl].astype(BF16),
                          moe_w2[jl].astype(BF16), tm=768, tf=256)

    y = _norm_call(x, norm_final[None, :], tm=512)
    y_prompt = y[:n_p].reshape(pb, pl_len, d)
    y_sample = y[n_p:].reshape(sb, sl_len, d)
    p_states = [jnp.stack(lst, axis=0) for lst in collected[0]]
    s_states = [jnp.stack(lst, axis=0) for lst in collected[1]]
    return (y_prompt, y_sample, *p_states, *s_states)
```

```python
import functools

import jax
import jax.numpy as jnp
import numpy as np
from jax import lax
from jax.experimental import pallas as pl
from jax.experimental.pallas import tpu as pltpu

F32 = jnp.float32
BF16 = jnp.bfloat16
I32 = jnp.int32

CHUNK = 64
HEAD_DIM = 64
ATT_HEADS = 8
KV_HEADS = 2
IDX_HEADS = 4
IDX_DIM = 64
TOPK_MAX = 256
ROPE_THETA = 10000.0
MIX = 512
SSM_HEADS = 8
SSM_HEAD_DIM = 64
SSM_GROUPS = 2
SSM_STATE = 64
SSM_BC = 2 * SSM_GROUPS * SSM_STATE
LRU_C = 8.0
N_EXPERTS = 8
NORM_EPS = 1e-6

LANES = 128
VMEM_LIMIT = 56 * 1024 * 1024

COL_Q, COL_Z, COL_XL, COL_GL, COL_GATES = 0, 512, 1024, 1536, 2048
COL_XS, COL_DT, COL_BC, COL_QI, COL_K, COL_V, COL_KIWI = 5120, 5632, 6144, 6400, 6656, 6784, 6912
PROJ_W = 7168

LOG2_E = 1.4426950408889634
NEG_BIG = -1e30
KEY_NEG_INF = -2139095041
INT_MAX = 2147483647


def _cparams(sem):
    return pltpu.CompilerParams(dimension_semantics=sem, vmem_limit_bytes=VMEM_LIMIT)


def _rms(x, g):
    ms = jnp.mean(x * x, axis=-1, keepdims=True)
    return x * lax.rsqrt(ms + NORM_EPS) * g


def _softplus(x):
    return jnp.maximum(x, 0.0) + jnp.log1p(jnp.exp(-jnp.abs(x)))


def _silu(x):
    return x * jax.nn.sigmoid(x)


def _norm_matmul_kernel(x_ref, g_ref, w_ref, o_ref, xn_ref):
    @pl.when(pl.program_id(1) == 0)
    def _():
        xn_ref[...] = _rms(x_ref[...], g_ref[...]).astype(BF16)

    o_ref[...] = jnp.dot(xn_ref[...], w_ref[...], preferred_element_type=F32)


def _norm_matmul(x, g, w, *, tm, tn):
    n, d = x.shape
    c = w.shape[1]
    return pl.pallas_call(
        _norm_matmul_kernel,
        grid=(n // tm, c // tn),
        in_specs=[pl.BlockSpec((tm, d), lambda i, j: (i, 0)),
                  pl.BlockSpec((1, d), lambda i, j: (0, 0)),
                  pl.BlockSpec((d, tn), lambda i, j: (0, j))],
        out_specs=pl.BlockSpec((tm, tn), lambda i, j: (i, j)),
        out_shape=jax.ShapeDtypeStruct((n, c), F32),
        scratch_shapes=[pltpu.VMEM((tm, d), BF16)],
        compiler_params=_cparams(("parallel", "arbitrary")),
        name="norm_in_proj",
    )(x, g, w)


def _rope_apply(x, cos, sin_signed, first_half):
    w = x.shape[1]
    reps = w // LANES
    if reps > 1:
        cos = jnp.concatenate([cos] * reps, axis=1)
        sin_signed = jnp.concatenate([sin_signed] * reps, axis=1)
        first_half = jnp.concatenate([first_half] * reps, axis=1)
    up = pltpu.roll(x, w - 32, axis=1)
    dn = pltpu.roll(x, 32, axis=1)
    return x * cos + jnp.where(first_half, up, dn) * sin_signed


def _rope_kernel(q_ref, qi_ref, k_ref, v_ref, kiwi_ref, cos_ref, sin_ref,
                 qo_ref, qio_ref, ko_ref, vo_ref, kio_ref, kbo_ref, vbo_ref, kibo_ref):
    cos = cos_ref[...]
    sin = sin_ref[...]
    lane = lax.broadcasted_iota(I32, cos.shape, 1)
    first_half = (lane % 64) < 32
    qo_ref[...] = _rope_apply(q_ref[...], cos, sin, first_half).astype(BF16)
    qio_ref[...] = _rope_apply(qi_ref[...], cos, sin, first_half).astype(BF16)
    k = _rope_apply(k_ref[...], cos, sin, first_half)
    ko_ref[...] = k
    kbo_ref[...] = k.astype(BF16)
    v = v_ref[...]
    vo_ref[...] = v
    vbo_ref[...] = v.astype(BF16)
    ki = _rope_apply(kiwi_ref[...], cos, sin, first_half)[:, :IDX_DIM]
    kio_ref[...] = ki
    kibo_ref[...] = ki.astype(BF16)


def _rope_call(proj, cos, sin, *, row0, nb, length, tl):
    nt = length // tl
    rb0 = row0 // tl
    n = nb * length

    def rows(b, j):
        return rb0 + b * nt + j

    def pspec(width, col):
        return pl.BlockSpec((tl, width), lambda b, j: (rows(b, j), col // width))

    def ospec(width):
        return pl.BlockSpec((tl, width), lambda b, j: (b * nt + j, 0))

    tspec = pl.BlockSpec((tl, LANES), lambda b, j: (j, 0))
    return pl.pallas_call(
        _rope_kernel,
        grid=(nb, nt),
        in_specs=[pspec(512, COL_Q), pspec(256, COL_QI), pspec(128, COL_K), pspec(128, COL_V),
                  pspec(128, COL_KIWI), tspec, tspec],
        out_specs=[ospec(512), ospec(256), ospec(128), ospec(128), ospec(IDX_DIM),
                   ospec(128), ospec(128), ospec(IDX_DIM)],
        out_shape=[jax.ShapeDtypeStruct((n, 512), BF16), jax.ShapeDtypeStruct((n, 256), BF16),
                   jax.ShapeDtypeStruct((n, 128), F32), jax.ShapeDtypeStruct((n, 128), F32),
                   jax.ShapeDtypeStruct((n, IDX_DIM), F32),
                   jax.ShapeDtypeStruct((n, 128), BF16), jax.ShapeDtypeStruct((n, 128), BF16),
                   jax.ShapeDtypeStruct((n, IDX_DIM), BF16)],
        compiler_params=_cparams(("parallel", "parallel")),
        name="rope",
    )(proj, proj, proj, proj, proj, cos, sin)


def _attn_kernel(q_ref, qi_ref, kiwi_ref, k_ref, v_ref, ki_ref, o_ref, key_ref, bias_ref,
                 m_ref, l_ref, acc_ref, *, tq, kb, s_valid, q_off, topk):
    i = pl.program_id(1)
    t0 = i * tq
    q_last = q_off + t0 + tq - 1
    n_adm = jnp.minimum((q_last // CHUNK + 1) * CHUNK, s_valid)
    nkb = (n_adm + kb - 1) // kb
    nsub = kb // LANES

    wi = kiwi_ref[:, IDX_DIM:IDX_DIM + IDX_HEADS]
    q_chunk = (q_off + t0 + lax.broadcasted_iota(I32, (tq, 1), 0)) // CHUNK
    nt_dims = (((1,), (1,)), ((), ()))

    def score_body(j, carry):
        off = pl.multiple_of(j * kb, kb)
        ki_blk = ki_ref[pl.ds(off, kb), :]
        sc = jnp.zeros((tq, kb), F32)
        for h in range(IDX_HEADS):
            qh = qi_ref[:, h * IDX_DIM:(h + 1) * IDX_DIM]
            s = lax.dot_general(qh, ki_blk, nt_dims, preferred_element_type=F32)
            sc = sc + jnp.maximum(s, 0.0) * wi[:, h:h + 1]
        kpos = off + lax.broadcasted_iota(I32, (1, kb), 1)
        adm = ((kpos // CHUNK) <= q_chunk) & (kpos < s_valid)
        sc = jnp.where(sc == 0.0, 0.0, sc)
        sc = jnp.where(adm, sc, -jnp.inf)
        bits = pltpu.bitcast(sc, I32)
        key_ref[j] = bits ^ ((bits >> 31) & INT_MAX)
        return carry

    lax.fori_loop(0, nkb, score_body, 0)

    def count_ge(thr):
        thr_b = jnp.broadcast_to(thr, (tq, LANES))

        def body(j, acc):
            blk = key_ref[j]
            for c in range(nsub):
                acc = acc + jnp.where(blk[:, c * LANES:(c + 1) * LANES] >= thr_b, 1, 0)
            return acc

        acc = lax.fori_loop(0, nkb, body, jnp.zeros((tq, LANES), I32))
        return jnp.sum(acc, axis=1, keepdims=True)

    def bis_body(_, lohi):
        lo, hi = lohi
        mid = (lo >> 1) + (hi >> 1) + (lo & hi & 1)
        ge = count_ge(mid) >= topk
        return jnp.where(ge, mid, lo), jnp.where(ge, hi, mid)

    lo0 = jnp.full((tq, 1), KEY_NEG_INF, I32)
    hi0 = jnp.full((tq, 1), INT_MAX, I32)
    thr, _ = lax.fori_loop(0, 32, bis_body, (lo0, hi0))
    cnt_ge = count_ge(thr)
    has_ties = jnp.max(cnt_ge) > topk

    @pl.when(jnp.logical_not(has_ties))
    def _():
        thr_eff = jnp.maximum(thr, KEY_NEG_INF + 1)

        def body(j, carry):
            bias_ref[j] = jnp.where(key_ref[j] >= thr_eff, 0.0, NEG_BIG)
            return carry

        lax.fori_loop(0, nkb, body, 0)

    @pl.when(has_ties)
    def _():
        cnt_gt = count_ge(thr + 1)
        need = (topk - cnt_gt).astype(F32)
        thr_ok = thr > KEY_NEG_INF
        r = lax.broadcasted_iota(I32, (kb, kb), 0)
        c = lax.broadcasted_iota(I32, (kb, kb), 1)
        upper = jnp.where(r < c, 1.0, 0.0).astype(BF16)

        def body(j, carry):
            key = key_ref[j]
            gt = key > thr
            eq = (key == thr) & thr_ok
            eqf = jnp.where(eq, 1.0, 0.0)
            rank = jnp.dot(eqf.astype(BF16), upper, preferred_element_type=F32) + carry
            sel = gt | (eq & (rank < need))
            bias_ref[j] = jnp.where(sel, 0.0, NEG_BIG)
            return carry + jnp.sum(eqf, axis=1, keepdims=True)

        lax.fori_loop(0, nkb, body, jnp.zeros((tq, 1), F32))

    hpg = ATT_HEADS // KV_HEADS

    def logits(j, h, kblk):
        qh = q_ref[:, h * HEAD_DIM:(h + 1) * HEAD_DIM]
        return lax.dot_general(qh, kblk, nt_dims, preferred_element_type=F32) + bias_ref[j]

    m_ref[...] = jnp.full(m_ref.shape, NEG_BIG, F32)

    def max_body(j, carry):
        off = pl.multiple_of(j * kb, kb)
        for g in range(KV_HEADS):
            kblk = k_ref[pl.ds(off, kb), g * HEAD_DIM:(g + 1) * HEAD_DIM]
            for hh in range(hpg):
                h = g * hpg + hh
                s = logits(j, h, kblk)
                mt = s[:, 0:LANES]
                for c in range(1, nsub):
                    mt = jnp.maximum(mt, s[:, c * LANES:(c + 1) * LANES])
                m_ref[h] = jnp.maximum(m_ref[h], mt)
        return carry

    lax.fori_loop(0, nkb, max_body, 0)
    for h in range(ATT_HEADS):
        m_ref[h] = jnp.broadcast_to(jnp.max(m_ref[h], axis=1, keepdims=True), (tq, LANES))

    l_ref[...] = jnp.zeros(l_ref.shape, F32)
    acc_ref[...] = jnp.zeros(acc_ref.shape, F32)

    def pv_body(j, carry):
        off = pl.multiple_of(j * kb, kb)
        for g in range(KV_HEADS):
            kblk = k_ref[pl.ds(off, kb), g * HEAD_DIM:(g + 1) * HEAD_DIM]
            vblk = v_ref[pl.ds(off, kb), g * HEAD_DIM:(g + 1) * HEAD_DIM]
            for hh in range(hpg):
                h = g * hpg + hh
                s = logits(j, h, kblk)
                mb = m_ref[h]
                ps = [jnp.exp2(s[:, c * LANES:(c + 1) * LANES] - mb) for c in range(nsub)]
                lsum = ps[0]
                for c in range(1, nsub):
                    lsum = lsum + ps[c]
                l_ref[h] += lsum
                p = jnp.concatenate(ps, axis=1).astype(BF16) if nsub > 1 else ps[0].astype(BF16)
                acc_ref[h] += jnp.dot(p, vblk, preferred_element_type=F32)
        return carry

    lax.fori_loop(0, nkb, pv_body, 0)
    for h in range(ATT_HEADS):
        l = jnp.sum(l_ref[h], axis=1, keepdims=True)
        o_ref[:, h * HEAD_DIM:(h + 1) * HEAD_DIM] = (acc_ref[h] / l).astype(BF16)


def _attn_call(q, qi, proj, k, v, ki, *, row0, nb, length, s_pad, s_valid, q_off, tq, kb, topk):
    nq = length // tq
    rb0 = row0 // tq
    nkb_max = s_pad // kb
    kern = functools.partial(_attn_kernel, tq=tq, kb=kb, s_valid=s_valid, q_off=q_off, topk=topk)
    return pl.pallas_call(
        kern,
        grid=(nb, nq),
        in_specs=[pl.BlockSpec((tq, 512), lambda b, i: (b * nq + i, 0)),
                  pl.BlockSpec((tq, 256), lambda b, i: (b * nq + i, 0)),
                  pl.BlockSpec((tq, 128), lambda b, i: (rb0 + b * nq + i, COL_KIWI // 128)),
                  pl.BlockSpec((s_pad, 128), lambda b, i: (b, 0)),
                  pl.BlockSpec((s_pad, 128), lambda b, i: (b, 0)),
                  pl.BlockSpec((s_pad, IDX_DIM), lambda b, i: (b, 0))],
        out_specs=pl.BlockSpec((tq, 512), lambda b, i: (b * nq + i, 0)),
        out_shape=jax.ShapeDtypeStruct((nb * length, 512), BF16),
        scratch_shapes=[pltpu.VMEM((nkb_max, tq, kb), I32), pltpu.VMEM((nkb_max, tq, kb), F32),
                        pltpu.VMEM((ATT_HEADS, tq, LANES), F32), pltpu.VMEM((ATT_HEADS, tq, LANES), F32),
                        pltpu.VMEM((ATT_HEADS, tq, HEAD_DIM), F32)],
        compiler_params=_cparams(("parallel", "arbitrary")),
        name="dsa_attention",
    )(q, qi, proj, k, v, ki)


def _ssm_kernel(z_ref, xs_ref, dt_ref, bc_ref, cx0_ref, cbc0_ref, s0_ref,
                cwx_ref, cwbc_ref, cbx_ref, cbbc_ref, dtb_ref, alog_ref, dsk_ref, nw_ref,
                y_ref, sT_ref, cxT_ref, cbcT_ref,
                xpx_ref, xpbc_ref, xc_ref, bcc_ref, dtc_ref, ypre_ref, st_ref,
                *, t_in, t_pad):
    j = pl.program_id(1)
    nj = pl.num_programs(1)
    q = CHUNK

    @pl.when(j == 0)
    def _():
        xpx_ref[0:8, :] = cx0_ref[...]
        xpbc_ref[0:8, :] = cbc0_ref[...]
        st_ref[...] = s0_ref[...]

    xpx_ref[8:8 + t_in, :] = xs_ref[...]
    xpbc_ref[8:8 + t_in, :] = bc_ref[...]

    def conv(xp_ref, w_ref, b_ref):
        y = b_ref[...]
        for tap in range(4):
            y = y + xp_ref[5 + tap:5 + tap + t_in, :] * w_ref[tap:tap + 1, :]
        return _silu(y)

    if t_pad > t_in:
        xc_ref[...] = jnp.zeros_like(xc_ref)
        bcc_ref[...] = jnp.zeros_like(bcc_ref)
        dtc_ref[...] = jnp.zeros_like(dtc_ref)
    xc_ref[0:t_in, :] = conv(xpx_ref, cwx_ref, cbx_ref)
    bcc_ref[0:t_in, :] = conv(xpbc_ref, cwbc_ref, cbbc_ref)
    dtc_ref[0:t_in, :] = _softplus(dt_ref[...] + dtb_ref[...])

    last_x = xpx_ref[t_in:t_in + 8, :]
    last_bc = xpbc_ref[t_in:t_in + 8, :]
    xpx_ref[0:8, :] = last_x
    xpbc_ref[0:8, :] = last_bc

    a_neg = -jnp.exp(alog_ref[...])
    li = lax.broadcasted_iota(I32, (q, q), 0)
    si = lax.broadcasted_iota(I32, (q, q), 1)
    tri = jnp.where(si <= li, 1.0, 0.0)
    ones = jnp.ones((q, q), F32)
    lane = lax.broadcasted_iota(I32, (q, MIX), 1)
    row = lax.broadcasted_iota(I32, (q, MIX), 0)
    s_of_lane = lane % q
    mask_t_le_s = jnp.where(row <= s_of_lane, 1.0, 0.0)
    causal = s_of_lane <= row
    rg = lax.broadcasted_iota(I32, (SSM_HEADS * q, 2 * SSM_STATE), 0) // (q * SSM_HEADS // SSM_GROUPS)
    cg = lax.broadcasted_iota(I32, (SSM_HEADS * q, 2 * SSM_STATE), 1) // SSM_STATE
    gmask = rg == cg
    rh = lax.broadcasted_iota(I32, (SSM_HEADS * q, MIX), 0) // q
    ch = lax.broadcasted_iota(I32, (SSM_HEADS * q, MIX), 1) // SSM_HEAD_DIM
    hmask = rh == ch
    r2 = lax.broadcasted_iota(I32, (2 * SSM_STATE, MIX), 0) // SSM_STATE
    c2 = lax.broadcasted_iota(I32, (2 * SSM_STATE, MIX), 1) // (MIX // SSM_GROUPS)
    g2mask = r2 == c2
    hp = lax.Precision.HIGHEST
    nt_dims = (((1,), (1,)), ((), ()))
    tn_dims = (((0,), (0,)), ((), ()))

    def chunk_body(c, carry):
        r0 = pl.multiple_of(c * q, q)
        xs = xc_ref[pl.ds(r0, q), :]
        dt = dtc_ref[pl.ds(r0, q), :]
        bmat = bcc_ref[pl.ds(r0, q), 0:2 * SSM_STATE]
        cmat = bcc_ref[pl.ds(r0, q), 2 * SSM_STATE:4 * SSM_STATE]
        a = dt * a_neg
        xdt = xs * dt
        acum = jnp.dot(tri, a, precision=hp, preferred_element_type=F32)
        rowt = jnp.dot(ones, a * mask_t_le_s, precision=hp, preferred_element_type=F32)
        decay_in = jnp.where(causal, jnp.exp(acum - rowt), 0.0)
        bexp = jnp.where(gmask, jnp.concatenate([bmat] * SSM_HEADS, axis=0), 0.0)
        cb = lax.dot_general(cmat.astype(BF16), bexp.astype(BF16), nt_dims,
                             preferred_element_type=F32)
        m = (cb * decay_in).astype(BF16)
        bdx = jnp.where(hmask, jnp.concatenate([xdt] * SSM_HEADS, axis=0), 0.0).astype(BF16)
        y_diag = jnp.dot(m, bdx, preferred_element_type=F32)
        st = st_ref[...]
        y_off = jnp.exp(acum) * jnp.dot(cmat.astype(BF16), st.astype(BF16),
                                        preferred_element_type=F32)
        a_end = acum[q - 1:q, :]
        xd = (xdt * jnp.exp(a_end - acum)).astype(BF16)
        upd = lax.dot_general(bmat.astype(BF16), xd, tn_dims, preferred_element_type=F32)
        st_ref[...] = jnp.exp(a_end) * st + jnp.where(g2mask, upd, 0.0)
        ypre_ref[pl.ds(r0, q), :] = y_diag + y_off
        return carry

    lax.fori_loop(0, t_pad // q, chunk_body, 0)

    xs = xc_ref[0:t_in, :]
    y = ypre_ref[0:t_in, :] + dsk_ref[...] * xs
    y = y * _silu(z_ref[...])
    half = MIX // SSM_GROUPS
    parts = []
    for g in range(SSM_GROUPS):
        yg = y[:, g * half:(g + 1) * half]
        parts.append(yg * lax.rsqrt(jnp.mean(yg * yg, axis=-1, keepdims=True) + NORM_EPS))
    y = jnp.concatenate(parts, axis=1) * nw_ref[...]
    y_ref[...] = y.astype(BF16)

    @pl.when(j == nj - 1)
    def _():
        sT_ref[...] = st_ref[...]
        cxT_ref[...] = last_x
        cbcT_ref[...] = last_bc


def _ssm_call(proj, cx0, cbc0, s0, wts, *, row0, nb, length, tl):
    nt = length // tl
    rb0 = row0 // tl
    t_pad = -(-tl // CHUNK) * CHUNK
    kern = functools.partial(_ssm_kernel, t_in=tl, t_pad=t_pad)

    def pspec(width, col):
        return pl.BlockSpec((tl, width), lambda b, j: (rb0 + b * nt + j, col // width))

    def bspec(r, c):
        return pl.BlockSpec((None, r, c), lambda b, j: (b, 0, 0))

    def wspec(r, c):
        return pl.BlockSpec((r, c), lambda b, j: (0, 0))

    return pl.pallas_call(
        kern,
        grid=(nb, nt),
        in_specs=[pspec(512, COL_Z), pspec(512, COL_XS), pspec(512, COL_DT), pspec(256, COL_BC),
                  bspec(8, MIX), bspec(8, SSM_BC), bspec(2 * SSM_STATE, MIX),
                  wspec(4, MIX), wspec(4, SSM_BC), wspec(1, MIX), wspec(1, SSM_BC),
                  wspec(1, MIX), wspec(1, MIX), wspec(1, MIX), wspec(1, MIX)],
        out_specs=[pl.BlockSpec((tl, MIX), lambda b, j: (b * nt + j, 0)),
                   bspec(2 * SSM_STATE, MIX), bspec(8, MIX), bspec(8, SSM_BC)],
        out_shape=[jax.ShapeDtypeStruct((nb * length, MIX), BF16),
                   jax.ShapeDtypeStruct((nb, 2 * SSM_STATE, MIX), F32),
                   jax.ShapeDtypeStruct((nb, 8, MIX), F32),
                   jax.ShapeDtypeStruct((nb, 8, SSM_BC), F32)],
        scratch_shapes=[pltpu.VMEM((tl + 8, MIX), F32), pltpu.VMEM((tl + 8, SSM_BC), F32),
                        pltpu.VMEM((t_pad, MIX), F32), pltpu.VMEM((t_pad, SSM_BC), F32),
                        pltpu.VMEM((t_pad, MIX), F32), pltpu.VMEM((t_pad, MIX), F32),
                        pltpu.VMEM((2 * SSM_STATE, MIX), F32)],
        compiler_params=_cparams(("parallel", "arbitrary")),
        name="ssd_branch",
    )(proj, proj, proj, proj, cx0, cbc0, s0, *wts)


def _lru_kernel(xl_ref, gl_ref, c0_ref, h0_ref, cw_ref, cb_ref, wa_ref, ba_ref, wx_ref, bx_ref,
                lam_ref, y_ref, hT_ref, cT_ref, xp_ref, a_ref, u_ref, hs_ref, h_ref, *, tl):
    j = pl.program_id(1)
    nj = pl.num_programs(1)

    @pl.when(j == 0)
    def _():
        xp_ref[0:8, :] = c0_ref[...]
        h_ref[...] = h0_ref[...]

    xp_ref[8:8 + tl, :] = xl_ref[...]
    xc = cb_ref[...]
    for tap in range(4):
        xc = xc + xp_ref[5 + tap:5 + tap + tl, :] * cw_ref[tap:tap + 1, :]
    last = xp_ref[tl:tl + 8, :]
    xp_ref[0:8, :] = last

    xcb = xc.astype(BF16)
    r = jax.nn.sigmoid(jnp.dot(xcb, wa_ref[...], preferred_element_type=F32) + ba_ref[...])
    i = jax.nn.sigmoid(jnp.dot(xcb, wx_ref[...], preferred_element_type=F32) + bx_ref[...])
    log_a = (-LRU_C * _softplus(-lam_ref[...])) * r
    a = jnp.exp(log_a)
    mult = jnp.sqrt(-jnp.tanh(log_a) * (a * a + 1.0))
    a_ref[...] = a
    u_ref[...] = mult * (i * xc)

    def step(t, h):
        h = a_ref[pl.ds(t, 1), :] * h + u_ref[pl.ds(t, 1), :]
        hs_ref[pl.ds(t, 1), :] = h
        return h

    h = lax.fori_loop(0, tl, step, h_ref[...], unroll=8)
    h_ref[...] = h
    y_ref[...] = (hs_ref[...] * jax.nn.gelu(gl_ref[...], approximate=True)).astype(BF16)

    @pl.when(j == nj - 1)
    def _():
        hT_ref[...] = h
        cT_ref[...] = last


def _lru_call(proj, c0, h0, wts, *, row0, nb, length, tl):
    nt = length // tl
    rb0 = row0 // tl
    kern = functools.partial(_lru_kernel, tl=tl)

    def pspec(col):
        return pl.BlockSpec((tl, MIX), lambda b, j: (rb0 + b * nt + j, col // MIX))

    def bspec(r):
        return pl.BlockSpec((None, r, MIX), lambda b, j: (b, 0, 0))

    def wspec(r):
        return pl.BlockSpec((r, MIX), lambda b, j: (0, 0))

    return pl.pallas_call(
        kern,
        grid=(nb, nt),
        in_specs=[pspec(COL_XL), pspec(COL_GL), bspec(8), bspec(1),
                  wspec(4), wspec(1), wspec(MIX), wspec(1), wspec(MIX), wspec(1), wspec(1)],
        out_specs=[pl.BlockSpec((tl, MIX), lambda b, j: (b * nt + j, 0)), bspec(1), bspec(8)],
        out_shape=[jax.ShapeDtypeStruct((nb * length, MIX), BF16),
                   jax.ShapeDtypeStruct((nb, 1, MIX), F32),
                   jax.ShapeDtypeStruct((nb, 8, MIX), F32)],
        scratch_shapes=[pltpu.VMEM((tl + 8, MIX), F32), pltpu.VMEM((tl, MIX), F32),
                        pltpu.VMEM((tl, MIX), F32), pltpu.VMEM((tl, MIX), F32),
                        pltpu.VMEM((1, MIX), F32)],
        compiler_params=_cparams(("parallel", "arbitrary")),
        name="rglru_branch",
    )(proj, proj, c0, h0, *wts)


def _merge_kernel(x_ref, att_ref, ssm_ref, lru_ref, g0_ref, g1_ref, g2_ref,
                  wa_ref, ws_ref, wl_ref, wo_ref, o_ref):
    merged = (jax.nn.sigmoid(g0_ref[...]) * jnp.dot(att_ref[...], wa_ref[...], preferred_element_type=F32)
              + jax.nn.sigmoid(g1_ref[...]) * jnp.dot(ssm_ref[...], ws_ref[...], preferred_element_type=F32)
              + jax.nn.sigmoid(g2_ref[...]) * jnp.dot(lru_ref[...], wl_ref[...], preferred_element_type=F32))
    o_ref[...] = x_ref[...] + jnp.dot(merged.astype(BF16), wo_ref[...], preferred_element_type=F32)


def _merge_call(x, att, ssm, lru, proj, wa, ws, wl, wo, *, tm):
    n, d = x.shape

    def rspec(width):
        return pl.BlockSpec((tm, width), lambda i: (i, 0))

    def gspec(k):
        return pl.BlockSpec((tm, d), lambda i: (i, COL_GATES // d + k))

    def wspec(r):
        return pl.BlockSpec((r, d), lambda i: (0, 0))

    return pl.pallas_call(
        _merge_kernel,
        grid=(n // tm,),
        in_specs=[rspec(d), rspec(MIX), rspec(MIX), rspec(MIX), gspec(0), gspec(1), gspec(2),
                  wspec(MIX), wspec(MIX), wspec(MIX), wspec(d)],
        out_specs=rspec(d),
        out_shape=jax.ShapeDtypeStruct((n, d), F32),
        compiler_params=_cparams(("parallel",)),
        name="branch_merge",
    )(x, att, ssm, lru, proj, proj, proj, wa, ws, wl, wo)


def _ffn_kernel(x_ref, g_ref, w1_ref, w3_ref, w2_ref, o_ref, xn_ref, acc_ref):
    f = pl.program_id(1)

    @pl.when(f == 0)
    def _():
        xn_ref[...] = _rms(x_ref[...], g_ref[...]).astype(BF16)
        acc_ref[...] = jnp.zeros_like(acc_ref)

    xn = xn_ref[...]
    h1 = jnp.dot(xn, w1_ref[...], preferred_element_type=F32)
    h3 = jnp.dot(xn, w3_ref[...], preferred_element_type=F32)
    h = (_silu(h1) * h3).astype(BF16)
    acc_ref[...] += jnp.dot(h, w2_ref[...], preferred_element_type=F32)

    @pl.when(f == pl.num_programs(1) - 1)
    def _():
        o_ref[...] = x_ref[...] + acc_ref[...]


def _ffn_call(x, g, w1, w3, w2, *, tm, tf):
    n, d = x.shape
    dff = w1.shape[1]
    return pl.pallas_call(
        _ffn_kernel,
        grid=(n // tm, dff // tf),
        in_specs=[pl.BlockSpec((tm, d), lambda i, f: (i, 0)),
                  pl.BlockSpec((1, d), lambda i, f: (0, 0)),
                  pl.BlockSpec((d, tf), lambda i, f: (0, f)),
                  pl.BlockSpec((d, tf), lambda i, f: (0, f)),
                  pl.BlockSpec((tf, d), lambda i, f: (f, 0))],
        out_specs=pl.BlockSpec((tm, d), lambda i, f: (i, 0)),
        out_shape=jax.ShapeDtypeStruct((n, d), F32),
        scratch_shapes=[pltpu.VMEM((tm, d), BF16), pltpu.VMEM((tm, d), F32)],
        compiler_params=_cparams(("parallel", "arbitrary")),
        name="swiglu_ffn",
    )(x, g, w1, w3, w2)


def _router_kernel(x_ref, g_ref, wr_ref, gate_ref):
    xn = _rms(x_ref[...], g_ref[...])
    logits = jnp.dot(xn, wr_ref[...], precision=lax.Precision.HIGHEST, preferred_element_type=F32)
    lane = lax.broadcasted_iota(I32, logits.shape, 1)
    logits = jnp.where(lane < N_EXPERTS, logits, -jnp.inf)
    m1 = jnp.max(logits, axis=1, keepdims=True)
    i1 = jnp.min(jnp.where(logits == m1, lane, LANES), axis=1, keepdims=True)
    rest = jnp.where(lane == i1, -jnp.inf, logits)
    m2 = jnp.max(rest, axis=1, keepdims=True)
    i2 = jnp.min(jnp.where(rest == m2, lane, LANES), axis=1, keepdims=True)
    e2 = jnp.exp(m2 - m1)
    den = 1.0 + e2
    gate_ref[...] = jnp.where(lane == i1, 1.0 / den, 0.0) + jnp.where(lane == i2, e2 / den, 0.0)


def _router_call(x, g, wr, *, tm):
    n, d = x.shape
    return pl.pallas_call(
        _router_kernel,
        grid=(n // tm,),
        in_specs=[pl.BlockSpec((tm, d), lambda i: (i, 0)),
                  pl.BlockSpec((1, d), lambda i: (0, 0)),
                  pl.BlockSpec((d, LANES), lambda i: (0, 0))],
        out_specs=pl.BlockSpec((tm, LANES), lambda i: (i, 0)),
        out_shape=jax.ShapeDtypeStruct((n, LANES), F32),
        compiler_params=_cparams(("parallel",)),
        name="moe_router",
    )(x, g, wr)


def _moe_kernel(x_ref, g_ref, gate_ref, w1_ref, w3_ref, w2_ref, o_ref, xn_ref, acc_ref):
    e = pl.program_id(1)
    f = pl.program_id(2)

    @pl.when((e == 0) & (f == 0))
    def _():
        xn_ref[...] = _rms(x_ref[...], g_ref[...]).astype(BF16)
        acc_ref[...] = jnp.zeros_like(acc_ref)

    gate = gate_ref[...]
    lane = lax.broadcasted_iota(I32, gate.shape, 1)
    ge = jnp.sum(jnp.where(lane == e, gate, 0.0), axis=1, keepdims=True)
    xn = xn_ref[...]
    h1 = jnp.dot(xn, w1_ref[...], preferred_element_type=F32)
    h3 = jnp.dot(xn, w3_ref[...], preferred_element_type=F32)
    h = (_silu(h1) * h3 * ge).astype(BF16)
    acc_ref[...] += jnp.dot(h, w2_ref[...], preferred_element_type=F32)

    @pl.when((e == pl.num_programs(1) - 1) & (f == pl.num_programs(2) - 1))
    def _():
        o_ref[...] = x_ref[...] + acc_ref[...]


def _moe_call(x, g, gate, w1, w3, w2, *, tm, tf):
    n, d = x.shape
    ne, _, dff = w1.shape
    return pl.pallas_call(
        _moe_kernel,
        grid=(n // tm, ne, dff // tf),
        in_specs=[pl.BlockSpec((tm, d), lambda i, e, f: (i, 0)),
                  pl.BlockSpec((1, d), lambda i, e, f: (0, 0)),
                  pl.BlockSpec((tm, LANES), lambda i, e, f: (i, 0)),
                  pl.BlockSpec((None, d, tf), lambda i, e, f: (e, 0, f)),
                  pl.BlockSpec((None, d, tf), lambda i, e, f: (e, 0, f)),
                  pl.BlockSpec((None, tf, d), lambda i, e, f: (e, f, 0))],
        out_specs=pl.BlockSpec((tm, d), lambda i, e, f: (i, 0)),
        out_shape=jax.ShapeDtypeStruct((n, d), F32),
        scratch_shapes=[pltpu.VMEM((tm, d), BF16), pltpu.VMEM((tm, d), F32)],
        compiler_params=_cparams(("parallel", "arbitrary", "arbitrary")),
        name="moe_swiglu",
    )(x, g, gate, w1, w3, w2)


def _norm_kernel(x_ref, g_ref, o_ref):
    o_ref[...] = _rms(x_ref[...], g_ref[...])


def _norm_call(x, g, *, tm):
    n, d = x.shape
    return pl.pallas_call(
        _norm_kernel,
        grid=(n // tm,),
        in_specs=[pl.BlockSpec((tm, d), lambda i: (i, 0)), pl.BlockSpec((1, d), lambda i: (0, 0))],
        out_specs=pl.BlockSpec((tm, d), lambda i: (i, 0)),
        out_shape=jax.ShapeDtypeStruct((n, d), F32),
        compiler_params=_cparams(("parallel",)),
        name="final_norm",
    )(x, g)


def _pack_w_in(w):
    d = w.shape[0]
    o = np.cumsum([0, 512, 128, 128, 256, 64, 4, 512, 768, 8, 512, 512, 3072])
    seg = lambda k: w[:, int(o[k]):int(o[k + 1])]
    q, k, v, qi, ki, wi, z, xbc, dt, xl, gl, gates = [seg(t) for t in range(12)]
    xs, bc = xbc[:, :MIX], xbc[:, MIX:]
    dt_exp = jnp.repeat(dt, SSM_HEAD_DIM, axis=1)
    kiwi = jnp.concatenate([ki, wi, jnp.zeros((d, LANES - IDX_DIM - IDX_HEADS), w.dtype)], axis=1)
    packed = jnp.concatenate(
        [q * (HEAD_DIM ** -0.5 * LOG2_E), z, xl, gl, gates, xs, dt_exp, bc, qi, k, v, kiwi,
         jnp.zeros((d, PROJ_W - COL_KIWI - LANES), w.dtype)], axis=1)
    return packed.astype(BF16)


def _block_diag(w):
    nblk, bw, _ = w.shape
    eye = jnp.eye(nblk, dtype=w.dtype)
    return jnp.einsum('kij,kl->kilj', w, eye).reshape(nblk * bw, nblk * bw)


def _rope_tables(pos):
    half = HEAD_DIM // 2
    inv = 1.0 / (ROPE_THETA ** (jnp.arange(half, dtype=F32) / half))
    ang = pos.astype(F32)[:, None] * inv[None, :]
    cos, sin = jnp.cos(ang), jnp.sin(ang)
    cos_t = jnp.concatenate([cos, cos, cos, cos], axis=1)
    sin_t = jnp.concatenate([-sin, sin, -sin, sin], axis=1)
    return cos_t, sin_t


def _pad_rows8(a):
    return jnp.pad(a, ((0, 0), (5, 0), (0, 0)))


def _state_to_s2(h):
    nb = h.shape[0]
    hg = h.reshape(nb, SSM_GROUPS, SSM_HEADS // SSM_GROUPS, SSM_HEAD_DIM, SSM_STATE)
    eye = jnp.eye(SSM_GROUPS, dtype=h.dtype)
    s2 = jnp.einsum('bgkpn,gf->bfngkp', hg, eye)
    return s2.reshape(nb, SSM_GROUPS * SSM_STATE, MIX)


def _s2_to_state(s2):
    nb = s2.shape[0]
    s6 = s2.reshape(nb, SSM_GROUPS, SSM_STATE, SSM_GROUPS, SSM_HEADS // SSM_GROUPS, SSM_HEAD_DIM)
    diag = jnp.stack([s6[:, g, :, g] for g in range(SSM_GROUPS)], axis=1)
    return diag.transpose(0, 1, 3, 4, 2).reshape(nb, SSM_HEADS, SSM_HEAD_DIM, SSM_STATE)


def _expand_heads(v):
    return jnp.repeat(v, SSM_HEAD_DIM)[None, :]


def kernel(x_prompt, x_sample, cache_k, cache_v, cache_kidx, state_ssm, state_ssm_conv, state_lru,
           state_lru_conv, norm_mix, norm_ffn, norm_final, w_in, ssm_conv_w, ssm_conv_b, ssm_dt_bias,
           ssm_a_log, ssm_d, ssm_norm, lru_conv_w, lru_conv_b, lru_wa, lru_ba, lru_wx, lru_bx, lru_lambda,
           w_att_out, w_ssm_out, w_lru_out, w_o, ffn_w1, ffn_w3, ffn_w2, moe_router, moe_w1, moe_w3, moe_w2):
    pb, pl_len, d = x_prompt.shape
    sb, sl_len, _ = x_sample.shape
    depth = w_in.shape[0]
    past = cache_k.shape[2]
    n_p = pb * pl_len
    n_s = sb * sl_len
    x = jnp.concatenate([x_prompt.reshape(n_p, d), x_sample.reshape(n_s, d)], axis=0)

    topk_p = min(TOPK_MAX, pl_len // 4)
    s_tot = past + sl_len
    topk_s = min(TOPK_MAX, s_tot // 4)
    kb_s = LANES
    s_pad = -(-s_tot // kb_s) * kb_s

    cos_p, sin_p = _rope_tables(jnp.arange(pl_len))
    cos_s, sin_s = _rope_tables(past + jnp.arange(sl_len))

    groups = (
        dict(row0=0, nb=pb, length=pl_len),
        dict(row0=n_p, nb=sb, length=sl_len),
    )
    tl_p, tl_s = 256, sl_len

    collected = [[[] for _ in range(7)] for _ in range(2)]
    for layer in range(depth):
        proj = _norm_matmul(x, norm_mix[layer][None, :], _pack_w_in(w_in[layer]), tm=512, tn=512)

        ssm_w = (ssm_conv_w[layer][:, :MIX], ssm_conv_w[layer][:, MIX:],
                 ssm_conv_b[layer][None, :MIX], ssm_conv_b[layer][None, MIX:],
                 _expand_heads(ssm_dt_bias[layer]), _expand_heads(ssm_a_log[layer]),
                 _expand_heads(ssm_d[layer]), ssm_norm[layer][None, :])
        lru_w = (lru_conv_w[layer], lru_conv_b[layer][None, :],
                 _block_diag(lru_wa[layer]).astype(BF16), lru_ba[layer][None, :],
                 _block_diag(lru_wx[layer]).astype(BF16), lru_bx[layer][None, :],
                 lru_lambda[layer][None, :])

        branch = [[], [], []]
        for gi, grp in enumerate(groups):
            nb, length = grp['nb'], grp['length']
            if gi == 0:
                tl, cos, sin = tl_p, cos_p, sin_p
                cx0 = jnp.zeros((nb, 8, MIX), F32)
                cbc0 = jnp.zeros((nb, 8, SSM_BC), F32)
                s0 = jnp.zeros((nb, 2 * SSM_STATE, MIX), F32)
                lc0 = jnp.zeros((nb, 8, MIX), F32)
                lh0 = jnp.zeros((nb, 1, MIX), F32)
            else:
                tl, cos, sin = tl_s, cos_s, sin_s
                conv0 = _pad_rows8(state_ssm_conv[layer])
                cx0, cbc0 = conv0[:, :, :MIX], conv0[:, :, MIX:]
                s0 = _state_to_s2(state_ssm[layer])
                lc0 = _pad_rows8(state_lru_conv[layer])
                lh0 = state_lru[layer][:, None, :]

            q_r, qi_r, k_r, v_r, ki_r, k_b, v_b, ki_b = _rope_call(proj, cos, sin, tl=tl, **grp)
            if gi == 0:
                att = _attn_call(q_r, qi_r, proj, k_b, v_b, ki_b, s_pad=length, s_valid=length,
                                 q_off=0, tq=256, kb=256, topk=topk_p, **grp)
            else:
                def cat(cache, new, width):
                    c = cache.reshape(nb, past, width).astype(BF16)
                    a = jnp.concatenate([c, new.reshape(nb, length, width)], axis=1)
                    a = jnp.pad(a, ((0, 0), (0, s_pad - s_tot), (0, 0)))
                    return a.reshape(nb * s_pad, width)
                att = _attn_call(q_r, qi_r, proj, cat(cache_k[layer], k_b, 128), cat(cache_v[layer], v_b, 128),
                                 cat(cache_kidx[layer], ki_b, IDX_DIM), s_pad=s_pad, s_valid=s_tot,
                                 q_off=past, tq=length, kb=kb_s, topk=topk_s, **grp)
            y_ssm, s_t, cx_t, cbc_t = _ssm_call(proj, cx0, cbc0, s0, ssm_w, tl=tl, **grp)
            y_lru, lh_t, lc_t = _lru_call(proj, lc0, lh0, lru_w, tl=tl, **grp)
            branch[0].append(att)
            branch[1].append(y_ssm)
            branch[2].append(y_lru)

            st = (k_r.reshape(nb, length, KV_HEADS, HEAD_DIM), v_r.reshape(nb, length, KV_HEADS, HEAD_DIM),
                  ki_r.reshape(nb, length, IDX_DIM), _s2_to_state(s_t),
                  jnp.concatenate([cx_t[:, 5:], cbc_t[:, 5:]], axis=2), lh_t[:, 0], lc_t[:, 5:])
            for lst, s in zip(collected[gi], st):
                lst.append(s)

        att, y_ssm, y_lru = [jnp.concatenate(bl, axis=0) for bl in branch]
        x = _merge_call(x, att, y_ssm, y_lru, proj, w_att_out[layer].astype(BF16),
                        w_ssm_out[layer].astype(BF16), w_lru_out[layer].astype(BF16),
                        w_o[layer].astype(BF16), tm=256)
        jl = layer // 2
        gf = norm_ffn[layer][None, :]
        if layer % 2 == 0:
            x = _ffn_call(x, gf, ffn_w1[jl].astype(BF16), ffn_w3[jl].astype(BF16),
                          ffn_w2[jl].astype(BF16), tm=768, tf=256)
        else:
            wr = jnp.pad(moe_router[jl], ((0, 0), (0, LANES - N_EXPERTS)))
            gate = _router_call(x, gf, wr, tm=512)
            x = _moe_call(x, gf, gate, moe_w1[jl].astype(BF16), moe_w3[jl].astype(BF16),
                          moe_w2[jl].astype(BF16), tm=768, tf=256)

    y = _norm_call(x, norm_final[None, :], tm=512)
    y_prompt = y[:n_p].reshape(pb, pl_len, d)
    y_sample = y[n_p:].reshape(sb, sl_len, d)
    p_states = [jnp.stack(lst, axis=0) for lst in collected[0]]
    s_states = [jnp.stack(lst, axis=0) for lst in collected[1]]
    return (y_prompt, y_sample, *p_states, *s_states)
```

```python
import functools

import jax
import jax.numpy as jnp
import numpy as np
from jax import lax
from jax.experimental import pallas as pl
from jax.experimental.pallas import tpu as pltpu

F32 = jnp.float32
BF16 = jnp.bfloat16
I32 = jnp.int32

CHUNK = 64
HEAD_DIM = 64
ATT_HEADS = 8
KV_HEADS = 2
IDX_HEADS = 4
IDX_DIM = 64
TOPK_MAX = 256
ROPE_THETA = 10000.0
MIX = 512
SSM_HEADS = 8
SSM_HEAD_DIM = 64
SSM_GROUPS = 2
SSM_STATE = 64
SSM_BC = 2 * SSM_GROUPS * SSM_STATE
LRU_C = 8.0
N_EXPERTS = 8
NORM_EPS = 1e-6

LANES = 128
VMEM_LIMIT = 56 * 1024 * 1024
MOE_TM = 1536
MOE_CH = 448

COL_Q, COL_Z, COL_XL, COL_GL, COL_GATES = 0, 512, 1024, 1536, 2048
COL_XS, COL_DT, COL_BC, COL_QI, COL_K, COL_V, COL_KIWI = 5120, 5632, 6144, 6400, 6656, 6784, 6912
PROJ_W = 7168

LOG2_E = 1.4426950408889634
SHIFT_MARGIN = 1.01
ROWSUM_FLOOR = 2.0 ** -60
NEG_BIG = -1e30
KEY_NEG_INF = -2139095041
INT_MAX = 2147483647
INT_MIN = -2147483648


def _cparams(sem):
    return pltpu.CompilerParams(dimension_semantics=sem, vmem_limit_bytes=VMEM_LIMIT)


def _rms(x, g):
    ms = jnp.mean(x * x, axis=-1, keepdims=True)
    return x * lax.rsqrt(ms + NORM_EPS) * g


def _softplus(x):
    return jnp.maximum(x, 0.0) + jnp.log1p(jnp.exp(-jnp.abs(x)))


def _silu(x):
    return x * jax.nn.sigmoid(x)


def _norm_matmul_kernel(x_ref, g_ref, w_ref, o_ref, xn_ref):
    @pl.when(pl.program_id(1) == 0)
    def _():
        xn_ref[...] = _rms(x_ref[...], g_ref[...]).astype(BF16)

    o_ref[...] = jnp.dot(xn_ref[...], w_ref[...], preferred_element_type=F32)


def _norm_matmul(x, g, w, *, tm, tn):
    n, d = x.shape
    c = w.shape[1]
    return pl.pallas_call(
        _norm_matmul_kernel,
        grid=(n // tm, c // tn),
        in_specs=[pl.BlockSpec((tm, d), lambda i, j: (i, 0)),
                  pl.BlockSpec((1, d), lambda i, j: (0, 0)),
                  pl.BlockSpec((d, tn), lambda i, j: (0, j))],
        out_specs=pl.BlockSpec((tm, tn), lambda i, j: (i, j)),
        out_shape=jax.ShapeDtypeStruct((n, c), F32),
        scratch_shapes=[pltpu.VMEM((tm, d), BF16)],
        compiler_params=_cparams(("parallel", "arbitrary")),
        name="norm_in_proj",
    )(x, g, w)


def _rope_apply(x, cos, sin_signed, first_half):
    w = x.shape[1]
    reps = w // LANES
    if reps > 1:
        cos = jnp.concatenate([cos] * reps, axis=1)
        sin_signed = jnp.concatenate([sin_signed] * reps, axis=1)
        first_half = jnp.concatenate([first_half] * reps, axis=1)
    up = pltpu.roll(x, w - 32, axis=1)
    dn = pltpu.roll(x, 32, axis=1)
    return x * cos + jnp.where(first_half, up, dn) * sin_signed


def _rope_kernel(q_ref, qi_ref, k_ref, v_ref, kiwi_ref, cos_ref, sin_ref,
                 qo_ref, qio_ref, ko_ref, vo_ref, kio_ref, kbo_ref, vbo_ref, kibo_ref):
    cos = cos_ref[...]
    sin = sin_ref[...]
    lane = lax.broadcasted_iota(I32, cos.shape, 1)
    first_half = (lane % 64) < 32
    qo_ref[...] = _rope_apply(q_ref[...], cos, sin, first_half).astype(BF16)
    qio_ref[...] = _rope_apply(qi_ref[...], cos, sin, first_half).astype(BF16)
    k = _rope_apply(k_ref[...], cos, sin, first_half)
    ko_ref[...] = k
    kbo_ref[...] = k.astype(BF16)
    v = v_ref[...]
    vo_ref[...] = v
    vbo_ref[...] = v.astype(BF16)
    ki = _rope_apply(kiwi_ref[...], cos, sin, first_half)[:, :IDX_DIM]
    kio_ref[...] = ki
    kibo_ref[...] = ki.astype(BF16)


def _rope_call(proj, cos, sin, *, row0, nb, length, tl):
    nt = length // tl
    rb0 = row0 // tl
    n = nb * length

    def rows(b, j):
        return rb0 + b * nt + j

    def pspec(width, col):
        return pl.BlockSpec((tl, width), lambda b, j: (rows(b, j), col // width))

    def ospec(width):
        return pl.BlockSpec((tl, width), lambda b, j: (b * nt + j, 0))

    tspec = pl.BlockSpec((tl, LANES), lambda b, j: (j, 0))
    return pl.pallas_call(
        _rope_kernel,
        grid=(nb, nt),
        in_specs=[pspec(512, COL_Q), pspec(256, COL_QI), pspec(128, COL_K), pspec(128, COL_V),
                  pspec(128, COL_KIWI), tspec, tspec],
        out_specs=[ospec(512), ospec(256), ospec(128), ospec(128), ospec(IDX_DIM),
                   ospec(128), ospec(128), ospec(IDX_DIM)],
        out_shape=[jax.ShapeDtypeStruct((n, 512), BF16), jax.ShapeDtypeStruct((n, 256), BF16),
                   jax.ShapeDtypeStruct((n, 128), F32), jax.ShapeDtypeStruct((n, 128), F32),
                   jax.ShapeDtypeStruct((n, IDX_DIM), F32),
                   jax.ShapeDtypeStruct((n, 128), BF16), jax.ShapeDtypeStruct((n, 128), BF16),
                   jax.ShapeDtypeStruct((n, IDX_DIM), BF16)],
        compiler_params=_cparams(("parallel", "parallel")),
        name="rope",
    )(proj, proj, proj, proj, proj, cos, sin)


def _attn_kernel(q_ref, qi_ref, kiwi_ref, k_ref, v_ref, ki_ref, o_ref, key_ref, bias_ref,
                 plane_ref, act_ref, m_ref, acc_ref, *, tq, kb, s_valid, q_off, topk):
    i = pl.program_id(1)
    t0 = i * tq
    q_last = q_off + t0 + tq - 1
    n_adm = jnp.minimum((q_last // CHUNK + 1) * CHUNK, s_valid)
    nkb = (n_adm + kb - 1) // kb
    nsub = kb // LANES

    wi = kiwi_ref[:, IDX_DIM:IDX_DIM + IDX_HEADS]
    q_chunk = (q_off + t0 + lax.broadcasted_iota(I32, (tq, 1), 0)) // CHUNK
    nt_dims = (((1,), (1,)), ((), ()))

    def score_body(j, kmax2):
        off = pl.multiple_of(j * kb, kb)
        kf = k_ref[pl.ds(off, kb), :].astype(F32)
        ksq = jnp.sum(kf * kf, axis=1, keepdims=True)
        kmax2 = jnp.maximum(kmax2, jnp.max(ksq, axis=0, keepdims=True))
        ki_blk = ki_ref[pl.ds(off, kb), :]
        sc = jnp.zeros((tq, kb), F32)
        for h in range(IDX_HEADS):
            qh = qi_ref[:, h * IDX_DIM:(h + 1) * IDX_DIM]
            s = lax.dot_general(qh, ki_blk, nt_dims, preferred_element_type=F32)
            sc = sc + jnp.maximum(s, 0.0) * wi[:, h:h + 1]
        kpos = off + lax.broadcasted_iota(I32, (1, kb), 1)
        adm = ((kpos // CHUNK) <= q_chunk) & (kpos < s_valid)
        sc = jnp.where(sc == 0.0, 0.0, sc)
        sc = jnp.where(adm, sc, -jnp.inf)
        bits = pltpu.bitcast(sc, I32)
        key_ref[j] = bits ^ ((bits >> 31) & INT_MAX)
        return kmax2

    kmax2 = lax.fori_loop(0, nkb, score_body, jnp.zeros((1, 1), F32))

    bpg = 32 // nsub
    ng = (nkb + bpg - 1) // bpg
    nslab = tq // 8

    def fill_body(j, carry):
        key_ref[j] = jnp.full((tq, kb), INT_MIN, I32)
        return carry

    lax.fori_loop(nkb, ng * bpg, fill_body, 0)

    def transpose_body(idx, carry):
        g = idx // nslab
        r0 = pl.multiple_of((idx % nslab) * 8, 8)
        xs = [key_ref[g * bpg + t // nsub, pl.ds(r0, 8), (t % nsub) * LANES:(t % nsub + 1) * LANES]
              ^ INT_MIN for t in range(32)]
        j, m = 16, 0x0000FFFF
        while j:
            k = 0
            while k < 32:
                t = (xs[k] ^ lax.shift_right_logical(xs[k + j], j)) & m
                xs[k] = xs[k] ^ t
                xs[k + j] = xs[k + j] ^ (t << j)
                k = (k + j + 1) & ~j
            j >>= 1
            m = (m ^ (m << j)) & 0xFFFFFFFF
        for b in range(32):
            plane_ref[g * 32 + b, pl.ds(r0, 8), :] = xs[b]
        return carry

    lax.fori_loop(0, ng * nslab, transpose_body, 0)

    def act_init(g, carry):
        act_ref[g] = jnp.full((tq, LANES), -1, I32)
        return carry

    lax.fori_loop(0, ng, act_init, 0)

    def bit_body(step, carry):
        rem, thr_u = carry

        def count_ones(g, acc):
            return acc + lax.population_count(act_ref[g] & plane_ref[g * 32 + step])

        n1 = jnp.sum(lax.fori_loop(0, ng, count_ones, jnp.zeros((tq, LANES), I32)),
                     axis=1, keepdims=True)
        take = n1 >= rem
        rem = jnp.where(take, rem, rem - n1)
        thr_u = jnp.where(take, thr_u | jnp.left_shift(jnp.int32(1), 31 - step), thr_u)

        def narrow(g, c):
            a = act_ref[g]
            w = a & plane_ref[g * 32 + step]
            act_ref[g] = jnp.where(take, w, a ^ w)
            return c

        lax.fori_loop(0, ng, narrow, 0)
        return rem, thr_u

    rem, thr_u = lax.fori_loop(0, 32, bit_body,
                               (jnp.full((tq, 1), topk, I32), jnp.zeros((tq, 1), I32)))
    thr = thr_u ^ INT_MIN

    def count_equal(g, acc):
        return acc + lax.population_count(act_ref[g])

    eq_cnt = jnp.sum(lax.fori_loop(0, ng, count_equal, jnp.zeros((tq, LANES), I32)),
                     axis=1, keepdims=True)
    has_ties = jnp.max(eq_cnt - rem) > 0

    @pl.when(jnp.logical_not(has_ties))
    def _():
        thr_eff = jnp.maximum(thr, KEY_NEG_INF + 1)

        def body(j, carry):
            bias_ref[j] = jnp.where(key_ref[j] >= thr_eff, 0.0, NEG_BIG)
            return carry

        lax.fori_loop(0, nkb, body, 0)

    @pl.when(has_ties)
    def _():
        need = rem.astype(F32)
        thr_ok = thr > KEY_NEG_INF
        r = lax.broadcasted_iota(I32, (kb, kb), 0)
        c = lax.broadcasted_iota(I32, (kb, kb), 1)
        upper = jnp.where(r < c, 1.0, 0.0).astype(BF16)

        def body(j, carry):
            key = key_ref[j]
            gt = key > thr
            eq = (key == thr) & thr_ok
            eqf = jnp.where(eq, 1.0, 0.0)
            rank = jnp.dot(eqf.astype(BF16), upper, preferred_element_type=F32) + carry
            sel = gt | (eq & (rank < need))
            bias_ref[j] = jnp.where(sel, 0.0, NEG_BIG)
            return carry + jnp.sum(eqf, axis=1, keepdims=True)

        lax.fori_loop(0, nkb, body, jnp.zeros((tq, 1), F32))

    hpg = ATT_HEADS // KV_HEADS

    def logits(j, h, kblk):
        qh = q_ref[:, h * HEAD_DIM:(h + 1) * HEAD_DIM]
        s = lax.dot_general(qh, kblk, nt_dims, preferred_element_type=F32)
        return s + bias_ref[j]

    def exp_and_values():
        acc_ref[...] = jnp.zeros(acc_ref.shape, F32)

        def pv_body(j, carry):
            off = pl.multiple_of(j * kb, kb)
            for g in range(KV_HEADS):
                kblk = k_ref[pl.ds(off, kb), g * HEAD_DIM:(g + 1) * HEAD_DIM]
                vblk = v_ref[pl.ds(off, kb), g * LANES:(g + 1) * LANES]
                for hh in range(hpg):
                    h = g * hpg + hh
                    s = logits(j, h, kblk)
                    mb = m_ref[h]
                    ps = [jnp.exp2(s[:, c * LANES:(c + 1) * LANES] - mb) for c in range(nsub)]
                    p = jnp.concatenate(ps, axis=1) if nsub > 1 else ps[0]
                    acc_ref[h] += jnp.dot(p.astype(BF16), vblk, preferred_element_type=F32)
            return carry

        lax.fori_loop(0, nkb, pv_body, 0)

    qf = q_ref[...].astype(F32)
    for h in range(ATT_HEADS):
        qh = qf[:, h * HEAD_DIM:(h + 1) * HEAD_DIM]
        bound = jnp.sqrt(jnp.sum(qh * qh, axis=1, keepdims=True) * kmax2) * SHIFT_MARGIN
        m_ref[h] = jnp.broadcast_to(bound, (tq, LANES))
    exp_and_values()
    lmin = jnp.min(acc_ref[0][:, HEAD_DIM:HEAD_DIM + 1])
    for h in range(1, ATT_HEADS):
        lmin = jnp.minimum(lmin, jnp.min(acc_ref[h][:, HEAD_DIM:HEAD_DIM + 1]))

    @pl.when(jnp.logical_not(lmin >= ROWSUM_FLOOR))
    def _():
        m_ref[...] = jnp.full(m_ref.shape, NEG_BIG, F32)

        def max_body(j, carry):
            off = pl.multiple_of(j * kb, kb)
            for g in range(KV_HEADS):
                kblk = k_ref[pl.ds(off, kb), g * HEAD_DIM:(g + 1) * HEAD_DIM]
                for hh in range(hpg):
                    h = g * hpg + hh
                    s = logits(j, h, kblk)
                    mt = s[:, 0:LANES]
                    for c in range(1, nsub):
                        mt = jnp.maximum(mt, s[:, c * LANES:(c + 1) * LANES])
                    m_ref[h] = jnp.maximum(m_ref[h], mt)
            return carry

        lax.fori_loop(0, nkb, max_body, 0)
        for h in range(ATT_HEADS):
            m_ref[h] = jnp.broadcast_to(jnp.max(m_ref[h], axis=1, keepdims=True), (tq, LANES))
        exp_and_values()

    for h in range(ATT_HEADS):
        acc = acc_ref[h]
        o_ref[:, h * HEAD_DIM:(h + 1) * HEAD_DIM] = (
            acc[:, :HEAD_DIM] / acc[:, HEAD_DIM:HEAD_DIM + 1]).astype(BF16)


def _attn_call(q, qi, proj, k, v, ki, *, row0, nb, length, s_pad, s_valid, q_off, tq, kb, topk):
    nq = length // tq
    rb0 = row0 // tq
    nkb_max = s_pad // kb
    bpg = 32 // (kb // LANES)
    ng_max = -(-nkb_max // bpg)
    kern = functools.partial(_attn_kernel, tq=tq, kb=kb, s_valid=s_valid, q_off=q_off, topk=topk)
    return pl.pallas_call(
        kern,
        grid=(nb, nq),
        in_specs=[pl.BlockSpec((tq, 512), lambda b, i: (b * nq + i, 0)),
                  pl.BlockSpec((tq, 256), lambda b, i: (b * nq + i, 0)),
                  pl.BlockSpec((tq, 128), lambda b, i: (rb0 + b * nq + i, COL_KIWI // 128)),
                  pl.BlockSpec((s_pad, 128), lambda b, i: (b, 0)),
                  pl.BlockSpec((s_pad, 2 * LANES), lambda b, i: (b, 0)),
                  pl.BlockSpec((s_pad, IDX_DIM), lambda b, i: (b, 0))],
        out_specs=pl.BlockSpec((tq, 512), lambda b, i: (b * nq + i, 0)),
        out_shape=jax.ShapeDtypeStruct((nb * length, 512), BF16),
        scratch_shapes=[pltpu.VMEM((ng_max * bpg, tq, kb), I32), pltpu.VMEM((nkb_max, tq, kb), F32),
                        pltpu.VMEM((ng_max * 32, tq, LANES), I32), pltpu.VMEM((ng_max, tq, LANES), I32),
                        pltpu.VMEM((ATT_HEADS, tq, LANES), F32), pltpu.VMEM((ATT_HEADS, tq, LANES), F32)],
        compiler_params=_cparams(("parallel", "arbitrary")),
        name="dsa_attention",
    )(q, qi, proj, k, v, ki)


def _ssm_kernel(z_ref, xs_ref, dt_ref, bc_ref, cx0_ref, cbc0_ref, s0_ref,
                cwx_ref, cwbc_ref, cbx_ref, cbbc_ref, dtb_ref, alog_ref, dsk_ref, nw_ref,
                y_ref, sT_ref, cxT_ref, cbcT_ref,
                xpx_ref, xpbc_ref, xc_ref, bcc_ref, dtc_ref, ypre_ref, st_ref,
                *, t_in, t_pad):
    j = pl.program_id(1)
    nj = pl.num_programs(1)
    q = CHUNK

    @pl.when(j == 0)
    def _():
        xpx_ref[0:8, :] = cx0_ref[...]
        xpbc_ref[0:8, :] = cbc0_ref[...]
        st_ref[...] = s0_ref[...]

    xpx_ref[8:8 + t_in, :] = xs_ref[...]
    xpbc_ref[8:8 + t_in, :] = bc_ref[...]

    def conv(xp_ref, w_ref, b_ref):
        y = b_ref[...]
        for tap in range(4):
            y = y + xp_ref[5 + tap:5 + tap + t_in, :] * w_ref[tap:tap + 1, :]
        return _silu(y)

    if t_pad > t_in:
        xc_ref[...] = jnp.zeros_like(xc_ref)
        bcc_ref[...] = jnp.zeros_like(bcc_ref)
        dtc_ref[...] = jnp.zeros_like(dtc_ref)
    xc_ref[0:t_in, :] = conv(xpx_ref, cwx_ref, cbx_ref)
    bcc_ref[0:t_in, :] = conv(xpbc_ref, cwbc_ref, cbbc_ref)
    dtc_ref[0:t_in, :] = _softplus(dt_ref[...] + dtb_ref[...])

    last_x = xpx_ref[t_in:t_in + 8, :]
    last_bc = xpbc_ref[t_in:t_in + 8, :]
    xpx_ref[0:8, :] = last_x
    xpbc_ref[0:8, :] = last_bc

    a_neg = -jnp.exp(alog_ref[...])
    li = lax.broadcasted_iota(I32, (q, q), 0)
    si = lax.broadcasted_iota(I32, (q, q), 1)
    tri = jnp.where(si <= li, 1.0, 0.0)
    ones = jnp.ones((q, q), F32)
    lane = lax.broadcasted_iota(I32, (q, MIX), 1)
    row = lax.broadcasted_iota(I32, (q, MIX), 0)
    s_of_lane = lane % q
    mask_t_le_s = jnp.where(row <= s_of_lane, 1.0, 0.0)
    causal = s_of_lane <= row
    rg = lax.broadcasted_iota(I32, (SSM_HEADS * q, 2 * SSM_STATE), 0) // (q * SSM_HEADS // SSM_GROUPS)
    cg = lax.broadcasted_iota(I32, (SSM_HEADS * q, 2 * SSM_STATE), 1) // SSM_STATE
    gmask = rg == cg
    rh = lax.broadcasted_iota(I32, (SSM_HEADS * q, MIX), 0) // q
    ch = lax.broadcasted_iota(I32, (SSM_HEADS * q, MIX), 1) // SSM_HEAD_DIM
    hmask = rh == ch
    r2 = lax.broadcasted_iota(I32, (2 * SSM_STATE, MIX), 0) // SSM_STATE
    c2 = lax.broadcasted_iota(I32, (2 * SSM_STATE, MIX), 1) // (MIX // SSM_GROUPS)
    g2mask = r2 == c2
    hp = lax.Precision.HIGHEST
    nt_dims = (((1,), (1,)), ((), ()))
    tn_dims = (((0,), (0,)), ((), ()))

    def chunk_body(c, carry):
        r0 = pl.multiple_of(c * q, q)
        xs = xc_ref[pl.ds(r0, q), :]
        dt = dtc_ref[pl.ds(r0, q), :]
        bmat = bcc_ref[pl.ds(r0, q), 0:2 * SSM_STATE]
        cmat = bcc_ref[pl.ds(r0, q), 2 * SSM_STATE:4 * SSM_STATE]
        a = dt * a_neg
        xdt = xs * dt
        acum = jnp.dot(tri, a, precision=hp, preferred_element_type=F32)
        rowt = jnp.dot(ones, a * mask_t_le_s, precision=hp, preferred_element_type=F32)
        decay_in = jnp.where(causal, jnp.exp(acum - rowt), 0.0)
        bexp = jnp.where(gmask, jnp.concatenate([bmat] * SSM_HEADS, axis=0), 0.0)
        cb = lax.dot_general(cmat.astype(BF16), bexp.astype(BF16), nt_dims,
                             preferred_element_type=F32)
        m = (cb * decay_in).astype(BF16)
        bdx = jnp.where(hmask, jnp.concatenate([xdt] * SSM_HEADS, axis=0), 0.0).astype(BF16)
        y_diag = jnp.dot(m, bdx, preferred_element_type=F32)
        st = st_ref[...]
        y_off = jnp.exp(acum) * jnp.dot(cmat.astype(BF16), st.astype(BF16),
                                        preferred_element_type=F32)
        a_end = acum[q - 1:q, :]
        xd = (xdt * jnp.exp(a_end - acum)).astype(BF16)
        upd = lax.dot_general(bmat.astype(BF16), xd, tn_dims, preferred_element_type=F32)
        st_ref[...] = jnp.exp(a_end) * st + jnp.where(g2mask, upd, 0.0)
        ypre_ref[pl.ds(r0, q), :] = y_diag + y_off
        return carry

    lax.fori_loop(0, t_pad // q, chunk_body, 0)

    xs = xc_ref[0:t_in, :]
    y = ypre_ref[0:t_in, :] + dsk_ref[...] * xs
    y = y * _silu(z_ref[...])
    half = MIX // SSM_GROUPS
    parts = []
    for g in range(SSM_GROUPS):
        yg = y[:, g * half:(g + 1) * half]
        parts.append(yg * lax.rsqrt(jnp.mean(yg * yg, axis=-1, keepdims=True) + NORM_EPS))
    y = jnp.concatenate(parts, axis=1) * nw_ref[...]
    y_ref[...] = y.astype(BF16)

    @pl.when(j == nj - 1)
    def _():
        sT_ref[...] = st_ref[...]
        cxT_ref[...] = last_x
        cbcT_ref[...] = last_bc


def _ssm_call(proj, cx0, cbc0, s0, wts, *, row0, nb, length, tl):
    nt = length // tl
    rb0 = row0 // tl
    t_pad = -(-tl // CHUNK) * CHUNK
    kern = functools.partial(_ssm_kernel, t_in=tl, t_pad=t_pad)

    def pspec(width, col):
        return pl.BlockSpec((tl, width), lambda b, j: (rb0 + b * nt + j, col // width))

    def bspec(r, c):
        return pl.BlockSpec((None, r, c), lambda b, j: (b, 0, 0))

    def wspec(r, c):
        return pl.BlockSpec((r, c), lambda b, j: (0, 0))

    return pl.pallas_call(
        kern,
        grid=(nb, nt),
        in_specs=[pspec(512, COL_Z), pspec(512, COL_XS), pspec(512, COL_DT), pspec(256, COL_BC),
                  bspec(8, MIX), bspec(8, SSM_BC), bspec(2 * SSM_STATE, MIX),
                  wspec(4, MIX), wspec(4, SSM_BC), wspec(1, MIX), wspec(1, SSM_BC),
                  wspec(1, MIX), wspec(1, MIX), wspec(1, MIX), wspec(1, MIX)],
        out_specs=[pl.BlockSpec((tl, MIX), lambda b, j: (b * nt + j, 0)),
                   bspec(2 * SSM_STATE, MIX), bspec(8, MIX), bspec(8, SSM_BC)],
        out_shape=[jax.ShapeDtypeStruct((nb * length, MIX), BF16),
                   jax.ShapeDtypeStruct((nb, 2 * SSM_STATE, MIX), F32),
                   jax.ShapeDtypeStruct((nb, 8, MIX), F32),
                   jax.ShapeDtypeStruct((nb, 8, SSM_BC), F32)],
        scratch_shapes=[pltpu.VMEM((tl + 8, MIX), F32), pltpu.VMEM((tl + 8, SSM_BC), F32),
                        pltpu.VMEM((t_pad, MIX), F32), pltpu.VMEM((t_pad, SSM_BC), F32),
                        pltpu.VMEM((t_pad, MIX), F32), pltpu.VMEM((t_pad, MIX), F32),
                        pltpu.VMEM((2 * SSM_STATE, MIX), F32)],
        compiler_params=_cparams(("parallel", "arbitrary")),
        name="ssd_branch",
    )(proj, proj, proj, proj, cx0, cbc0, s0, *wts)


def _lru_kernel(xl_ref, gl_ref, c0_ref, h0_ref, cw_ref, cb_ref, wa_ref, ba_ref, wx_ref, bx_ref,
                lam_ref, y_ref, hT_ref, cT_ref, xp_ref, a_ref, u_ref, hs_ref, h_ref, *, tl):
    j = pl.program_id(1)
    nj = pl.num_programs(1)

    @pl.when(j == 0)
    def _():
        xp_ref[0:8, :] = c0_ref[...]
        h_ref[...] = h0_ref[...]

    xp_ref[8:8 + tl, :] = xl_ref[...]
    xc = cb_ref[...]
    for tap in range(4):
        xc = xc + xp_ref[5 + tap:5 + tap + tl, :] * cw_ref[tap:tap + 1, :]
    last = xp_ref[tl:tl + 8, :]
    xp_ref[0:8, :] = last

    xcb = xc.astype(BF16)
    r = jax.nn.sigmoid(jnp.dot(xcb, wa_ref[...], preferred_element_type=F32) + ba_ref[...])
    i = jax.nn.sigmoid(jnp.dot(xcb, wx_ref[...], preferred_element_type=F32) + bx_ref[...])
    log_a = (-LRU_C * _softplus(-lam_ref[...])) * r
    a = jnp.exp(log_a)
    mult = jnp.sqrt(-jnp.tanh(log_a) * (a * a + 1.0))
    a_ref[...] = a
    u_ref[...] = mult * (i * xc)

    def step(t, h):
        h = a_ref[pl.ds(t, 1), :] * h + u_ref[pl.ds(t, 1), :]
        hs_ref[pl.ds(t, 1), :] = h
        return h

    h = lax.fori_loop(0, tl, step, h_ref[...], unroll=8)
    h_ref[...] = h
    y_ref[...] = (hs_ref[...] * jax.nn.gelu(gl_ref[...], approximate=True)).astype(BF16)

    @pl.when(j == nj - 1)
    def _():
        hT_ref[...] = h
        cT_ref[...] = last


def _lru_call(proj, c0, h0, wts, *, row0, nb, length, tl):
    nt = length // tl
    rb0 = row0 // tl
    kern = functools.partial(_lru_kernel, tl=tl)

    def pspec(col):
        return pl.BlockSpec((tl, MIX), lambda b, j: (rb0 + b * nt + j, col // MIX))

    def bspec(r):
        return pl.BlockSpec((None, r, MIX), lambda b, j: (b, 0, 0))

    def wspec(r):
        return pl.BlockSpec((r, MIX), lambda b, j: (0, 0))

    return pl.pallas_call(
        kern,
        grid=(nb, nt),
        in_specs=[pspec(COL_XL), pspec(COL_GL), bspec(8), bspec(1),
                  wspec(4), wspec(1), wspec(MIX), wspec(1), wspec(MIX), wspec(1), wspec(1)],
        out_specs=[pl.BlockSpec((tl, MIX), lambda b, j: (b * nt + j, 0)), bspec(1), bspec(8)],
        out_shape=[jax.ShapeDtypeStruct((nb * length, MIX), BF16),
                   jax.ShapeDtypeStruct((nb, 1, MIX), F32),
                   jax.ShapeDtypeStruct((nb, 8, MIX), F32)],
        scratch_shapes=[pltpu.VMEM((tl + 8, MIX), F32), pltpu.VMEM((tl, MIX), F32),
                        pltpu.VMEM((tl, MIX), F32), pltpu.VMEM((tl, MIX), F32),
                        pltpu.VMEM((1, MIX), F32)],
        compiler_params=_cparams(("parallel", "arbitrary")),
        name="rglru_branch",
    )(proj, proj, c0, h0, *wts)


def _merge_kernel(x_ref, att_ref, ssm_ref, lru_ref, g0_ref, g1_ref, g2_ref,
                  wa_ref, ws_ref, wl_ref, wo_ref, o_ref):
    merged = (jax.nn.sigmoid(g0_ref[...]) * jnp.dot(att_ref[...], wa_ref[...], preferred_element_type=F32)
              + jax.nn.sigmoid(g1_ref[...]) * jnp.dot(ssm_ref[...], ws_ref[...], preferred_element_type=F32)
              + jax.nn.sigmoid(g2_ref[...]) * jnp.dot(lru_ref[...], wl_ref[...], preferred_element_type=F32))
    o_ref[...] = x_ref[...] + jnp.dot(merged.astype(BF16), wo_ref[...], preferred_element_type=F32)


def _merge_call(x, att, ssm, lru, proj, wa, ws, wl, wo, *, tm):
    n, d = x.shape

    def rspec(width):
        return pl.BlockSpec((tm, width), lambda i: (i, 0))

    def gspec(k):
        return pl.BlockSpec((tm, d), lambda i: (i, COL_GATES // d + k))

    def wspec(r):
        return pl.BlockSpec((r, d), lambda i: (0, 0))

    return pl.pallas_call(
        _merge_kernel,
        grid=(n // tm,),
        in_specs=[rspec(d), rspec(MIX), rspec(MIX), rspec(MIX), gspec(0), gspec(1), gspec(2),
                  wspec(MIX), wspec(MIX), wspec(MIX), wspec(d)],
        out_specs=rspec(d),
        out_shape=jax.ShapeDtypeStruct((n, d), F32),
        compiler_params=_cparams(("parallel",)),
        name="branch_merge",
    )(x, att, ssm, lru, proj, proj, proj, wa, ws, wl, wo)


def _ffn_kernel(x_ref, g_ref, w1_ref, w3_ref, w2_ref, o_ref, xn_ref, acc_ref):
    f = pl.program_id(1)

    @pl.when(f == 0)
    def _():
        xn_ref[...] = _rms(x_ref[...], g_ref[...]).astype(BF16)
        acc_ref[...] = jnp.zeros_like(acc_ref)

    xn = xn_ref[...]
    h1 = jnp.dot(xn, w1_ref[...], preferred_element_type=F32)
    h3 = jnp.dot(xn, w3_ref[...], preferred_element_type=F32)
    h = (_silu(h1) * h3).astype(BF16)
    acc_ref[...] += jnp.dot(h, w2_ref[...], preferred_element_type=F32)

    @pl.when(f == pl.num_programs(1) - 1)
    def _():
        o_ref[...] = x_ref[...] + acc_ref[...]


def _ffn_call(x, g, w1, w3, w2, *, tm, tf):
    n, d = x.shape
    dff = w1.shape[1]
    return pl.pallas_call(
        _ffn_kernel,
        grid=(n // tm, dff // tf),
        in_specs=[pl.BlockSpec((tm, d), lambda i, f: (i, 0)),
                  pl.BlockSpec((1, d), lambda i, f: (0, 0)),
                  pl.BlockSpec((d, tf), lambda i, f: (0, f)),
                  pl.BlockSpec((d, tf), lambda i, f: (0, f)),
                  pl.BlockSpec((tf, d), lambda i, f: (f, 0))],
        out_specs=pl.BlockSpec((tm, d), lambda i, f: (i, 0)),
        out_shape=jax.ShapeDtypeStruct((n, d), F32),
        scratch_shapes=[pltpu.VMEM((tm, d), BF16), pltpu.VMEM((tm, d), F32)],
        compiler_params=_cparams(("parallel", "arbitrary")),
        name="swiglu_ffn",
    )(x, g, w1, w3, w2)


def _router_kernel(x_ref, g_ref, wr_ref, gate_ref, xn_ref):
    xn = _rms(x_ref[...], g_ref[...])
    xn_ref[...] = xn.astype(BF16)
    logits = jnp.dot(xn, wr_ref[...], precision=lax.Precision.HIGHEST, preferred_element_type=F32)
    lane = lax.broadcasted_iota(I32, logits.shape, 1)
    logits = jnp.where(lane < N_EXPERTS, logits, -jnp.inf)
    m1 = jnp.max(logits, axis=1, keepdims=True)
    i1 = jnp.min(jnp.where(logits == m1, lane, LANES), axis=1, keepdims=True)
    rest = jnp.where(lane == i1, -jnp.inf, logits)
    m2 = jnp.max(rest, axis=1, keepdims=True)
    i2 = jnp.min(jnp.where(rest == m2, lane, LANES), axis=1, keepdims=True)
    e2 = jnp.exp(m2 - m1)
    den = 1.0 + e2
    gate_ref[...] = jnp.where(lane == i1, 1.0 / den, 0.0) + jnp.where(lane == i2, e2 / den, 0.0)


def _router_call(x, g, wr, *, tm):
    n, d = x.shape
    return pl.pallas_call(
        _router_kernel,
        grid=(n // tm,),
        in_specs=[pl.BlockSpec((tm, d), lambda i: (i, 0)),
                  pl.BlockSpec((1, d), lambda i: (0, 0)),
                  pl.BlockSpec((d, LANES), lambda i: (0, 0))],
        out_specs=[pl.BlockSpec((tm, LANES), lambda i: (i, 0)), pl.BlockSpec((tm, d), lambda i: (i, 0))],
        out_shape=[jax.ShapeDtypeStruct((n, LANES), F32), jax.ShapeDtypeStruct((n, d), BF16)],
        compiler_params=_cparams(("parallel",)),
        name="moe_router",
    )(x, g, wr)


def _moe_expert_kernel(nch_ref, slot_ref, xn_ref, w1_ref, w3_ref, w2_ref, yc_ref, xg_ref, yacc_ref,
                       *, ch):
    e = pl.program_id(0)
    t = pl.program_id(1)
    f = pl.program_id(2)
    nch = nch_ref[t * pl.num_programs(0) + e]

    @pl.when(f == 0)
    def _():
        srow = slot_ref[pl.ds(e, 1), :]

        def gather(c, carry):
            rows = c * ch + lax.broadcasted_iota(I32, (ch, 1), 0)
            onehot = jnp.where(srow == rows, 1.0, 0.0).astype(BF16)
            xg_ref[c] = jnp.dot(onehot, xn_ref[...], preferred_element_type=F32).astype(BF16)
            yacc_ref[c] = jnp.zeros((ch, yacc_ref.shape[2]), F32)
            return carry

        lax.fori_loop(0, nch, gather, 0)

    def ffn(c, carry):
        xg = xg_ref[c]
        h1 = jnp.dot(xg, w1_ref[...], preferred_element_type=F32)
        h3 = jnp.dot(xg, w3_ref[...], preferred_element_type=F32)
        h = (_silu(h1) * h3).astype(BF16)
        yacc_ref[c] += jnp.dot(h, w2_ref[...], preferred_element_type=F32)
        return carry

    lax.fori_loop(0, nch, ffn, 0)

    @pl.when(f == pl.num_programs(2) - 1)
    def _():
        def emit(c, carry):
            yc_ref[pl.ds(pl.multiple_of(c * ch, ch), ch), :] = yacc_ref[c].astype(BF16)
            return carry

        def clear(c, carry):
            yc_ref[pl.ds(pl.multiple_of(c * ch, ch), ch), :] = jnp.zeros((ch, yc_ref.shape[1]), BF16)
            return carry

        lax.fori_loop(0, nch, emit, 0)
        lax.fori_loop(nch, yc_ref.shape[0] // ch, clear, 0)


def _moe_expert_call(nch, slot_exp, xn, w1, w3, w2, *, tm, tf, ch):
    n, d = xn.shape
    ne, _, dff = w1.shape
    nt = n // tm
    nch_max = -(-tm // ch)
    kern = functools.partial(_moe_expert_kernel, ch=ch)
    return pl.pallas_call(
        kern,
        grid_spec=pltpu.PrefetchScalarGridSpec(
            num_scalar_prefetch=1,
            grid=(ne, nt, dff // tf),
            in_specs=[pl.BlockSpec((None, ne, tm), lambda e, t, f, nch: (t, 0, 0)),
                      pl.BlockSpec((tm, d), lambda e, t, f, nch: (t, 0)),
                      pl.BlockSpec((None, d, tf), lambda e, t, f, nch: (e, 0, f)),
                      pl.BlockSpec((None, d, tf), lambda e, t, f, nch: (e, 0, f)),
                      pl.BlockSpec((None, tf, d), lambda e, t, f, nch: (e, f, 0))],
            out_specs=pl.BlockSpec((None, None, nch_max * ch, d), lambda e, t, f, nch: (e, t, 0, 0)),
            scratch_shapes=[pltpu.VMEM((nch_max, ch, d), BF16), pltpu.VMEM((nch_max, ch, d), F32)]),
        out_shape=jax.ShapeDtypeStruct((ne, nt, nch_max * ch, d), BF16),
        compiler_params=_cparams(("parallel", "parallel", "arbitrary")),
        name="moe_experts",
    )(nch, slot_exp, xn, w1, w3, w2)


def _moe_combine_kernel(nch_ref, x_ref, slot_ref, gate_ref, yc_ref, g_ref, o_ref, *, ch, final_norm):
    t = pl.program_id(0)
    e = pl.program_id(1)
    nch = nch_ref[t * pl.num_programs(1) + e]

    @pl.when(e == 0)
    def _():
        o_ref[...] = x_ref[...]

    lane = lax.broadcasted_iota(I32, slot_ref.shape, 1)
    scol = jnp.sum(jnp.where(lane == e, slot_ref[...], 0.0), axis=1, keepdims=True)
    gcol = jnp.sum(jnp.where(lane == e, gate_ref[...], 0.0), axis=1, keepdims=True)

    def scatter(c, carry):
        cols = (c * ch + lax.broadcasted_iota(I32, (1, ch), 1)).astype(F32)
        onehot = jnp.where(scol == cols, 1.0, 0.0).astype(BF16)
        yc = yc_ref[pl.ds(pl.multiple_of(c * ch, ch), ch), :]
        o_ref[...] += gcol * jnp.dot(onehot, yc, preferred_element_type=F32)
        return carry

    lax.fori_loop(0, nch, scatter, 0)

    if final_norm:
        @pl.when(e == pl.num_programs(1) - 1)
        def _():
            o_ref[...] = _rms(o_ref[...], g_ref[...])


def _moe_combine_call(nch, x, slot_tok, gate, yc, g_final, *, tm, ch, final_norm):
    n, d = x.shape
    ne, nt, rows, _ = yc.shape
    kern = functools.partial(_moe_combine_kernel, ch=ch, final_norm=final_norm)
    return pl.pallas_call(
        kern,
        grid_spec=pltpu.PrefetchScalarGridSpec(
            num_scalar_prefetch=1,
            grid=(nt, ne),
            in_specs=[pl.BlockSpec((tm, d), lambda t, e, nch: (t, 0)),
                      pl.BlockSpec((tm, LANES), lambda t, e, nch: (t, 0)),
                      pl.BlockSpec((tm, LANES), lambda t, e, nch: (t, 0)),
                      pl.BlockSpec((None, None, rows, d), lambda t, e, nch: (e, t, 0, 0)),
                      pl.BlockSpec((1, d), lambda t, e, nch: (0, 0))],
            out_specs=pl.BlockSpec((tm, d), lambda t, e, nch: (t, 0))),
        out_shape=jax.ShapeDtypeStruct((n, d), F32),
        compiler_params=_cparams(("parallel", "arbitrary")),
        name="moe_combine",
    )(nch, x, slot_tok, gate, yc, g_final)


def _moe_routing_tables(gate, *, tm, ch):
    n = gate.shape[0]
    nt = n // tm
    routed = (gate[:, :N_EXPERTS] != 0.0).astype(I32).reshape(nt, tm, N_EXPERTS)
    rank = jnp.cumsum(routed, axis=1) - routed
    slot = jnp.where(routed > 0, rank, -1)
    counts = jnp.sum(routed, axis=1)
    nch = ((counts + ch - 1) // ch).reshape(nt * N_EXPERTS).astype(I32)
    slot_exp = slot.transpose(0, 2, 1)
    slot_tok = jnp.pad(slot.reshape(n, N_EXPERTS).astype(F32), ((0, 0), (0, LANES - N_EXPERTS)),
                       constant_values=-1.0)
    return nch, slot_exp, slot_tok


def _norm_kernel(x_ref, g_ref, o_ref):
    o_ref[...] = _rms(x_ref[...], g_ref[...])


def _norm_call(x, g, *, tm):
    n, d = x.shape
    return pl.pallas_call(
        _norm_kernel,
        grid=(n // tm,),
        in_specs=[pl.BlockSpec((tm, d), lambda i: (i, 0)), pl.BlockSpec((1, d), lambda i: (0, 0))],
        out_specs=pl.BlockSpec((tm, d), lambda i: (i, 0)),
        out_shape=jax.ShapeDtypeStruct((n, d), F32),
        compiler_params=_cparams(("parallel",)),
        name="final_norm",
    )(x, g)


def _pack_w_in(w):
    d = w.shape[0]
    o = np.cumsum([0, 512, 128, 128, 256, 64, 4, 512, 768, 8, 512, 512, 3072])
    seg = lambda k: w[:, int(o[k]):int(o[k + 1])]
    q, k, v, qi, ki, wi, z, xbc, dt, xl, gl, gates = [seg(t) for t in range(12)]
    xs, bc = xbc[:, :MIX], xbc[:, MIX:]
    dt_exp = jnp.repeat(dt, SSM_HEAD_DIM, axis=1)
    kiwi = jnp.concatenate([ki, wi, jnp.zeros((d, LANES - IDX_DIM - IDX_HEADS), w.dtype)], axis=1)
    packed = jnp.concatenate(
        [q * (HEAD_DIM ** -0.5 * LOG2_E), z, xl, gl, gates, xs, dt_exp, bc, qi, k, v, kiwi,
         jnp.zeros((d, PROJ_W - COL_KIWI - LANES), w.dtype)], axis=1)
    return packed.astype(BF16)


def _block_diag(w):
    nblk, bw, _ = w.shape
    eye = jnp.eye(nblk, dtype=w.dtype)
    return jnp.einsum('kij,kl->kilj', w, eye).reshape(nblk * bw, nblk * bw)


def _rope_tables(pos):
    half = HEAD_DIM // 2
    inv = 1.0 / (ROPE_THETA ** (jnp.arange(half, dtype=F32) / half))
    ang = pos.astype(F32)[:, None] * inv[None, :]
    cos, sin = jnp.cos(ang), jnp.sin(ang)
    cos_t = jnp.concatenate([cos, cos, cos, cos], axis=1)
    sin_t = jnp.concatenate([-sin, sin, -sin, sin], axis=1)
    return cos_t, sin_t


def _pad_rows8(a):
    return jnp.pad(a, ((0, 0), (5, 0), (0, 0)))


def _state_to_s2(h):
    nb = h.shape[0]
    hg = h.reshape(nb, SSM_GROUPS, SSM_HEADS // SSM_GROUPS, SSM_HEAD_DIM, SSM_STATE)
    eye = jnp.eye(SSM_GROUPS, dtype=h.dtype)
    s2 = jnp.einsum('bgkpn,gf->bfngkp', hg, eye)
    return s2.reshape(nb, SSM_GROUPS * SSM_STATE, MIX)


def _s2_to_state(s2):
    nb = s2.shape[0]
    s6 = s2.reshape(nb, SSM_GROUPS, SSM_STATE, SSM_GROUPS, SSM_HEADS // SSM_GROUPS, SSM_HEAD_DIM)
    diag = jnp.stack([s6[:, g, :, g] for g in range(SSM_GROUPS)], axis=1)
    return diag.transpose(0, 1, 3, 4, 2).reshape(nb, SSM_HEADS, SSM_HEAD_DIM, SSM_STATE)


def _with_ones_column(v):
    n = v.shape[0]
    e = jnp.zeros((n, LANES - HEAD_DIM), v.dtype).at[:, 0].set(1)
    return jnp.concatenate([v[:, :HEAD_DIM], e, v[:, HEAD_DIM:], e], axis=1)


def _expand_heads(v):
    return jnp.repeat(v, SSM_HEAD_DIM)[None, :]


def kernel(x_prompt, x_sample, cache_k, cache_v, cache_kidx, state_ssm, state_ssm_conv, state_lru,
           state_lru_conv, norm_mix, norm_ffn, norm_final, w_in, ssm_conv_w, ssm_conv_b, ssm_dt_bias,
           ssm_a_log, ssm_d, ssm_norm, lru_conv_w, lru_conv_b, lru_wa, lru_ba, lru_wx, lru_bx, lru_lambda,
           w_att_out, w_ssm_out, w_lru_out, w_o, ffn_w1, ffn_w3, ffn_w2, moe_router, moe_w1, moe_w3, moe_w2):
    pb, pl_len, d = x_prompt.shape
    sb, sl_len, _ = x_sample.shape
    depth = w_in.shape[0]
    past = cache_k.shape[2]
    n_p = pb * pl_len
    n_s = sb * sl_len
    x = jnp.concatenate([x_prompt.reshape(n_p, d), x_sample.reshape(n_s, d)], axis=0)

    topk_p = min(TOPK_MAX, pl_len // 4)
    s_tot = past + sl_len
    topk_s = min(TOPK_MAX, s_tot // 4)
    kb_s = LANES
    s_pad = -(-s_tot // kb_s) * kb_s

    cos_p, sin_p = _rope_tables(jnp.arange(pl_len))
    cos_s, sin_s = _rope_tables(past + jnp.arange(sl_len))

    groups = (
        dict(row0=0, nb=pb, length=pl_len),
        dict(row0=n_p, nb=sb, length=sl_len),
    )
    tl_p, tl_s = 256, sl_len

    collected = [[[] for _ in range(7)] for _ in range(2)]
    final_fused = False
    for layer in range(depth):
        proj = _norm_matmul(x, norm_mix[layer][None, :], _pack_w_in(w_in[layer]), tm=1536, tn=1024)

        ssm_w = (ssm_conv_w[layer][:, :MIX], ssm_conv_w[layer][:, MIX:],
                 ssm_conv_b[layer][None, :MIX], ssm_conv_b[layer][None, MIX:],
                 _expand_heads(ssm_dt_bias[layer]), _expand_heads(ssm_a_log[layer]),
                 _expand_heads(ssm_d[layer]), ssm_norm[layer][None, :])
        lru_w = (lru_conv_w[layer], lru_conv_b[layer][None, :],
                 _block_diag(lru_wa[layer]).astype(BF16), lru_ba[layer][None, :],
                 _block_diag(lru_wx[layer]).astype(BF16), lru_bx[layer][None, :],
                 lru_lambda[layer][None, :])

        branch = [[], [], []]
        for gi, grp in enumerate(groups):
            nb, length = grp['nb'], grp['length']
            if gi == 0:
                tl, cos, sin = tl_p, cos_p, sin_p
                cx0 = jnp.zeros((nb, 8, MIX), F32)
                cbc0 = jnp.zeros((nb, 8, SSM_BC), F32)
                s0 = jnp.zeros((nb, 2 * SSM_STATE, MIX), F32)
                lc0 = jnp.zeros((nb, 8, MIX), F32)
                lh0 = jnp.zeros((nb, 1, MIX), F32)
            else:
                tl, cos, sin = tl_s, cos_s, sin_s
                conv0 = _pad_rows8(state_ssm_conv[layer])
                cx0, cbc0 = conv0[:, :, :MIX], conv0[:, :, MIX:]
                s0 = _state_to_s2(state_ssm[layer])
                lc0 = _pad_rows8(state_lru_conv[layer])
                lh0 = state_lru[layer][:, None, :]

            q_r, qi_r, k_r, v_r, ki_r, k_b, v_b, ki_b = _rope_call(proj, cos, sin, tl=tl, **grp)
            if gi == 0:
                att = _attn_call(q_r, qi_r, proj, k_b, _with_ones_column(v_b), ki_b, s_pad=length,
                                 s_valid=length, q_off=0, tq=256, kb=256, topk=topk_p, **grp)
            else:
                def cat(cache, new, width):
                    c = cache.reshape(nb, past, width).astype(BF16)
                    a = jnp.concatenate([c, new.reshape(nb, length, width)], axis=1)
                    a = jnp.pad(a, ((0, 0), (0, s_pad - s_tot), (0, 0)))
                    return a.reshape(nb * s_pad, width)
                att = _attn_call(q_r, qi_r, proj, cat(cache_k[layer], k_b, 128),
                                 _with_ones_column(cat(cache_v[layer], v_b, 128)),
                                 cat(cache_kidx[layer], ki_b, IDX_DIM), s_pad=s_pad, s_valid=s_tot,
                                 q_off=past, tq=length, kb=kb_s, topk=topk_s, **grp)
            y_ssm, s_t, cx_t, cbc_t = _ssm_call(proj, cx0, cbc0, s0, ssm_w, tl=tl, **grp)
            y_lru, lh_t, lc_t = _lru_call(proj, lc0, lh0, lru_w, tl=tl, **grp)
            branch[0].append(att)
            branch[1].append(y_ssm)
            branch[2].append(y_lru)

            st = (k_r.reshape(nb, length, KV_HEADS, HEAD_DIM), v_r.reshape(nb, length, KV_HEADS, HEAD_DIM),
                  ki_r.reshape(nb, length, IDX_DIM), _s2_to_state(s_t),
                  jnp.concatenate([cx_t[:, 5:], cbc_t[:, 5:]], axis=2), lh_t[:, 0], lc_t[:, 5:])
            for lst, s in zip(collected[gi], st):
                lst.append(s)

        att, y_ssm, y_lru = [jnp.concatenate(bl, axis=0) for bl in branch]
        x = _merge_call(x, att, y_ssm, y_lru, proj, w_att_out[layer].astype(BF16),
                        w_ssm_out[layer].astype(BF16), w_lru_out[layer].astype(BF16),
                        w_o[layer].astype(BF16), tm=256)
        jl = layer // 2
        gf = norm_ffn[layer][None, :]
        if layer % 2 == 0:
            x = _ffn_call(x, gf, ffn_w1[jl].astype(BF16), ffn_w3[jl].astype(BF16),
                          ffn_w2[jl].astype(BF16), tm=768, tf=1408)
        else:
            wr = jnp.pad(moe_router[jl], ((0, 0), (0, LANES - N_EXPERTS)))
            gate, xn = _router_call(x, gf, wr, tm=512)
            nch, slot_exp, slot_tok = _moe_routing_tables(gate, tm=MOE_TM, ch=MOE_CH)
            yc = _moe_expert_call(nch, slot_exp, xn, moe_w1[jl].astype(BF16), moe_w3[jl].astype(BF16),
                                  moe_w2[jl].astype(BF16), tm=MOE_TM, tf=1408, ch=MOE_CH)
            final_fused = layer == depth - 1
            x = _moe_combine_call(nch, x, slot_tok, gate, yc, norm_final[None, :], tm=MOE_TM,
                                  ch=MOE_CH, final_norm=final_fused)

    y = x if final_fused else _norm_call(x, norm_final[None, :], tm=512)
    y_prompt = y[:n_p].reshape(pb, pl_len, d)
    y_sample = y[n_p:].reshape(sb, sl_len, d)
    p_states = [jnp.stack(lst, axis=0) for lst in collected[0]]
    s_states = [jnp.stack(lst, axis=0) for lst in collected[1]]
    return (y_prompt, y_sample, *p_states, *s_states)
```

```python
import functools

import jax
import jax.numpy as jnp
import numpy as np
from jax import lax
from jax.experimental import pallas as pl
from jax.experimental.pallas import tpu as pltpu

F32 = jnp.float32
BF16 = jnp.bfloat16
I32 = jnp.int32

CHUNK = 64
HEAD_DIM = 64
ATT_HEADS = 8
KV_HEADS = 2
IDX_HEADS = 4
IDX_DIM = 64
TOPK_MAX = 256
ROPE_THETA = 10000.0
MIX = 512
SSM_HEADS = 8
SSM_HEAD_DIM = 64
SSM_GROUPS = 2
SSM_STATE = 64
SSM_BC = 2 * SSM_GROUPS * SSM_STATE
LRU_C = 8.0
N_EXPERTS = 8
NORM_EPS = 1e-6

LANES = 128
VMEM_LIMIT = 56 * 1024 * 1024
MOE_TM = 1536
MOE_CH = 448

COL_Q, COL_Z, COL_XL, COL_GL, COL_GATES = 0, 512, 1024, 1536, 2048
COL_XS, COL_DT, COL_BC, COL_QI, COL_K, COL_V, COL_KIWI = 5120, 5632, 6144, 6400, 6656, 6784, 6912
PROJ_W = 7168

LOG2_E = 1.4426950408889634
SHIFT_MARGIN = 1.01
ROWSUM_FLOOR = 2.0 ** -60
NEG_BIG = -1e30
KEY_NEG_INF = -2139095041
INT_MAX = 2147483647
INT_MIN = -2147483648


def _cparams(sem):
    return pltpu.CompilerParams(dimension_semantics=sem, vmem_limit_bytes=VMEM_LIMIT)


def _rms(x, g):
    ms = jnp.mean(x * x, axis=-1, keepdims=True)
    return x * lax.rsqrt(ms + NORM_EPS) * g


def _softplus(x):
    return jnp.maximum(x, 0.0) + jnp.log1p(jnp.exp(-jnp.abs(x)))


def _silu(x):
    return x * jax.nn.sigmoid(x)


def _norm_matmul_kernel(x_ref, g_ref, w_ref, o_ref, xn_ref):
    @pl.when(pl.program_id(1) == 0)
    def _():
        xn_ref[...] = _rms(x_ref[...], g_ref[...]).astype(BF16)

    o_ref[...] = jnp.dot(xn_ref[...], w_ref[...], preferred_element_type=F32)


def _norm_matmul(x, g, w, *, tm, tn):
    n, d = x.shape
    c = w.shape[1]
    return pl.pallas_call(
        _norm_matmul_kernel,
        grid=(n // tm, c // tn),
        in_specs=[pl.BlockSpec((tm, d), lambda i, j: (i, 0)),
                  pl.BlockSpec((1, d), lambda i, j: (0, 0)),
                  pl.BlockSpec((d, tn), lambda i, j: (0, j))],
        out_specs=pl.BlockSpec((tm, tn), lambda i, j: (i, j)),
        out_shape=jax.ShapeDtypeStruct((n, c), F32),
        scratch_shapes=[pltpu.VMEM((tm, d), BF16)],
        compiler_params=_cparams(("parallel", "arbitrary")),
        name="norm_in_proj",
    )(x, g, w)


def _rope_apply(x, cos, sin_signed, first_half):
    w = x.shape[1]
    reps = w // LANES
    if reps > 1:
        cos = jnp.concatenate([cos] * reps, axis=1)
        sin_signed = jnp.concatenate([sin_signed] * reps, axis=1)
        first_half = jnp.concatenate([first_half] * reps, axis=1)
    up = pltpu.roll(x, w - 32, axis=1)
    dn = pltpu.roll(x, 32, axis=1)
    return x * cos + jnp.where(first_half, up, dn) * sin_signed


def _rope_kernel(q_ref, qi_ref, k_ref, v_ref, kiwi_ref, cos_ref, sin_ref,
                 qo_ref, qio_ref, ko_ref, vo_ref, kio_ref, kbo_ref, vbo_ref, kibo_ref):
    cos = cos_ref[...]
    sin = sin_ref[...]
    lane = lax.broadcasted_iota(I32, cos.shape, 1)
    first_half = (lane % 64) < 32
    qo_ref[...] = _rope_apply(q_ref[...], cos, sin, first_half).astype(BF16)
    qio_ref[...] = _rope_apply(qi_ref[...], cos, sin, first_half).astype(BF16)
    k = _rope_apply(k_ref[...], cos, sin, first_half)
    ko_ref[...] = k
    kbo_ref[...] = k.astype(BF16)
    v = v_ref[...]
    vo_ref[...] = v
    vbo_ref[...] = v.astype(BF16)
    ki = _rope_apply(kiwi_ref[...], cos, sin, first_half)[:, :IDX_DIM]
    kio_ref[...] = ki
    kibo_ref[...] = ki.astype(BF16)


def _rope_call(proj, cos, sin, *, row0, nb, length, tl):
    nt = length // tl
    rb0 = row0 // tl
    n = nb * length

    def rows(b, j):
        return rb0 + b * nt + j

    def pspec(width, col):
        return pl.BlockSpec((tl, width), lambda b, j: (rows(b, j), col // width))

    def ospec(width):
        return pl.BlockSpec((tl, width), lambda b, j: (b * nt + j, 0))

    tspec = pl.BlockSpec((tl, LANES), lambda b, j: (j, 0))
    return pl.pallas_call(
        _rope_kernel,
        grid=(nb, nt),
        in_specs=[pspec(512, COL_Q), pspec(256, COL_QI), pspec(128, COL_K), pspec(128, COL_V),
                  pspec(128, COL_KIWI), tspec, tspec],
        out_specs=[ospec(512), ospec(256), ospec(128), ospec(128), ospec(IDX_DIM),
                   ospec(128), ospec(128), ospec(IDX_DIM)],
        out_shape=[jax.ShapeDtypeStruct((n, 512), BF16), jax.ShapeDtypeStruct((n, 256), BF16),
                   jax.ShapeDtypeStruct((n, 128), F32), jax.ShapeDtypeStruct((n, 128), F32),
                   jax.ShapeDtypeStruct((n, IDX_DIM), F32),
                   jax.ShapeDtypeStruct((n, 128), BF16), jax.ShapeDtypeStruct((n, 128), BF16),
                   jax.ShapeDtypeStruct((n, IDX_DIM), BF16)],
        compiler_params=_cparams(("parallel", "parallel")),
        name="rope",
    )(proj, proj, proj, proj, proj, cos, sin)


def _attn_kernel(q_ref, qi_ref, kiwi_ref, k_ref, v_ref, ki_ref, o_ref, key_ref, bias_ref,
                 plane_ref, act_ref, m_ref, acc_ref, knb_ref, *, tq, kb, s_valid, q_off, topk):
    i = pl.program_id(1)
    t0 = i * tq
    q_last = q_off + t0 + tq - 1
    n_adm = jnp.minimum((q_last // CHUNK + 1) * CHUNK, s_valid)
    nkb = (n_adm + kb - 1) // kb
    nsub = kb // LANES

    wi = kiwi_ref[:, IDX_DIM:IDX_DIM + IDX_HEADS]
    q_chunk = (q_off + t0 + lax.broadcasted_iota(I32, (tq, 1), 0)) // CHUNK
    nt_dims = (((1,), (1,)), ((), ()))

    @pl.when(i == 0)
    def _():
        if plane_ref.shape[0] > 32:
            plane_ref[32:] = jnp.zeros((plane_ref.shape[0] - 32,) + plane_ref.shape[1:], I32)
        for j in range(knb_ref.shape[0]):
            kf = k_ref[j * kb:(j + 1) * kb, :].astype(F32)
            ksq = jnp.sum(kf * kf, axis=1, keepdims=True)
            knb_ref[j] = jnp.broadcast_to(jnp.max(ksq, axis=0, keepdims=True), knb_ref.shape[1:])

    def kmax_body(j, m):
        return jnp.maximum(m, knb_ref[j])

    kmax2 = lax.fori_loop(0, nkb, kmax_body, jnp.zeros(knb_ref.shape[1:], F32))[0:1, 0:1]

    wib = [jnp.broadcast_to(wi[:, h:h + 1], (tq, LANES)) for h in range(IDX_HEADS)]

    def score_block(j, masked):
        off = pl.multiple_of(j * kb, kb)
        ki_blk = ki_ref[pl.ds(off, kb), :]
        sc = None
        for h in range(IDX_HEADS):
            qh = qi_ref[:, h * IDX_DIM:(h + 1) * IDX_DIM]
            s = lax.dot_general(qh, ki_blk, nt_dims, preferred_element_type=F32)
            wb = wib[h] if nsub == 1 else jnp.concatenate([wib[h]] * nsub, axis=1)
            term = jnp.maximum(s, 0.0) * wb
            sc = term if sc is None else sc + term
        sc = jnp.where(sc == 0.0, 0.0, sc)
        if masked:
            kpos = off + lax.broadcasted_iota(I32, (1, kb), 1)
            adm = ((kpos // CHUNK) <= q_chunk) & (kpos < s_valid)
            sc = jnp.where(adm, sc, -jnp.inf)
        bits = pltpu.bitcast(sc, I32)
        key_ref[j] = bits ^ ((bits >> 31) & INT_MAX)

    nfull = jnp.minimum(((q_off + t0) // CHUNK + 1) * CHUNK, s_valid) // kb

    def score_full(j, carry):
        score_block(j, False)
        return carry

    def score_edge(j, carry):
        score_block(j, True)
        return carry

    lax.fori_loop(0, nfull, score_full, 0)
    lax.fori_loop(nfull, nkb, score_edge, 0)

    bpg = 32 // nsub
    ng = (nkb + bpg - 1) // bpg
    nslab = tq // 8

    def fill_body(j, carry):
        key_ref[j] = jnp.full((tq, kb), INT_MIN, I32)
        return carry

    lax.fori_loop(nkb, ng * bpg, fill_body, 0)

    def transpose_body(idx, carry):
        g = idx // nslab
        r0 = pl.multiple_of((idx % nslab) * 8, 8)
        xs = [key_ref[g * bpg + t // nsub, pl.ds(r0, 8), (t % nsub) * LANES:(t % nsub + 1) * LANES]
              ^ INT_MIN for t in range(32)]
        j, m = 16, 0x0000FFFF
        while j:
            k = 0
            while k < 32:
                t = (xs[k] ^ lax.shift_right_logical(xs[k + j], j)) & m
                xs[k] = xs[k] ^ t
                xs[k + j] = xs[k + j] ^ (t << j)
                k = (k + j + 1) & ~j
            j >>= 1
            m = (m ^ (m << j)) & 0xFFFFFFFF
        for b in range(32):
            plane_ref[g * 32 + b, pl.ds(r0, 8), :] = xs[b]
        return carry

    lax.fori_loop(0, ng * nslab, transpose_body, 0)

    ng_max = act_ref.shape[0]
    for g in range(ng_max):
        act_ref[g] = jnp.broadcast_to(jnp.where(g < ng, -1, 0), (tq, LANES)).astype(I32)

    nhalf = 2 if tq % 16 == 0 and tq >= 128 else 1
    hrows = tq // nhalf

    def bit_body(step, carry):
        out = []
        for hf in range(nhalf):
            rem, thr_u = carry[2 * hf], carry[2 * hf + 1]
            rs = slice(hf * hrows, (hf + 1) * hrows)
            n1l = jnp.zeros((hrows, LANES), I32)
            for g in range(ng_max):
                n1l = n1l + lax.population_count(act_ref[g, rs, :] & plane_ref[g * 32 + step, rs, :])
            n1 = jnp.sum(n1l, axis=1, keepdims=True)
            take = n1 >= rem
            rem = jnp.where(take, rem, rem - n1)
            thr_u = jnp.where(take, thr_u | jnp.left_shift(jnp.int32(1), 31 - step), thr_u)
            for g in range(ng_max):
                a = act_ref[g, rs, :]
                w = a & plane_ref[g * 32 + step, rs, :]
                act_ref[g, rs, :] = jnp.where(take, w, a ^ w)
            out += [rem, thr_u]
        return tuple(out)

    init = (jnp.full((hrows, 1), topk, I32), jnp.zeros((hrows, 1), I32)) * nhalf
    fin = lax.fori_loop(0, 32, bit_body, init)
    rem = fin[0] if nhalf == 1 else jnp.concatenate(fin[0::2], axis=0)
    thr_u = fin[1] if nhalf == 1 else jnp.concatenate(fin[1::2], axis=0)
    thr = thr_u ^ INT_MIN

    eq_l = jnp.zeros((tq, LANES), I32)
    for g in range(ng_max):
        eq_l = eq_l + lax.population_count(act_ref[g])
    eq_cnt = jnp.sum(eq_l, axis=1, keepdims=True)
    has_ties = jnp.max(eq_cnt - rem) > 0

    @pl.when(jnp.logical_not(has_ties))
    def _():
        thr_eff = jnp.maximum(thr, KEY_NEG_INF + 1)

        def body(j, carry):
            bias_ref[j] = jnp.where(key_ref[j] >= thr_eff, 0.0, NEG_BIG)
            return carry

        lax.fori_loop(0, nkb, body, 0)

    @pl.when(has_ties)
    def _():
        need = rem.astype(F32)
        thr_ok = thr > KEY_NEG_INF
        r = lax.broadcasted_iota(I32, (kb, kb), 0)
        c = lax.broadcasted_iota(I32, (kb, kb), 1)
        upper = jnp.where(r < c, 1.0, 0.0).astype(BF16)

        def body(j, carry):
            key = key_ref[j]
            gt = key > thr
            eq = (key == thr) & thr_ok
            eqf = jnp.where(eq, 1.0, 0.0)
            rank = jnp.dot(eqf.astype(BF16), upper, preferred_element_type=F32) + carry
            sel = gt | (eq & (rank < need))
            bias_ref[j] = jnp.where(sel, 0.0, NEG_BIG)
            return carry + jnp.sum(eqf, axis=1, keepdims=True)

        lax.fori_loop(0, nkb, body, jnp.zeros((tq, 1), F32))

    hpg = ATT_HEADS // KV_HEADS

    def logits(j, h, kblk):
        qh = q_ref[:, h * HEAD_DIM:(h + 1) * HEAD_DIM]
        s = lax.dot_general(qh, kblk, nt_dims, preferred_element_type=F32)
        return s + bias_ref[j]

    def exp_and_values():
        acc_ref[...] = jnp.zeros(acc_ref.shape, F32)

        def pv_body(j, carry):
            off = pl.multiple_of(j * kb, kb)
            for g in range(KV_HEADS):
                kblk = k_ref[pl.ds(off, kb), g * HEAD_DIM:(g + 1) * HEAD_DIM]
                vblk = v_ref[pl.ds(off, kb), g * LANES:(g + 1) * LANES]
                for hh in range(hpg):
                    h = g * hpg + hh
                    s = logits(j, h, kblk)
                    mb = m_ref[h]
                    ps = [jnp.exp2(s[:, c * LANES:(c + 1) * LANES] - mb) for c in range(nsub)]
                    p = jnp.concatenate(ps, axis=1) if nsub > 1 else ps[0]
                    acc_ref[h] += jnp.dot(p.astype(BF16), vblk, preferred_element_type=F32)
            return carry

        lax.fori_loop(0, nkb, pv_body, 0)

    qf = q_ref[...].astype(F32)
    for h in range(ATT_HEADS):
        qh = qf[:, h * HEAD_DIM:(h + 1) * HEAD_DIM]
        bound = jnp.sqrt(jnp.sum(qh * qh, axis=1, keepdims=True) * kmax2) * SHIFT_MARGIN
        m_ref[h] = jnp.broadcast_to(bound, (tq, LANES))
    exp_and_values()
    lmin = jnp.min(acc_ref[0][:, HEAD_DIM:HEAD_DIM + 1])
    for h in range(1, ATT_HEADS):
        lmin = jnp.minimum(lmin, jnp.min(acc_ref[h][:, HEAD_DIM:HEAD_DIM + 1]))

    @pl.when(jnp.logical_not(lmin >= ROWSUM_FLOOR))
    def _():
        m_ref[...] = jnp.full(m_ref.shape, NEG_BIG, F32)

        def max_body(j, carry):
            off = pl.multiple_of(j * kb, kb)
            for g in range(KV_HEADS):
                kblk = k_ref[pl.ds(off, kb), g * HEAD_DIM:(g + 1) * HEAD_DIM]
                for hh in range(hpg):
                    h = g * hpg + hh
                    s = logits(j, h, kblk)
                    mt = s[:, 0:LANES]
                    for c in range(1, nsub):
                        mt = jnp.maximum(mt, s[:, c * LANES:(c + 1) * LANES])
                    m_ref[h] = jnp.maximum(m_ref[h], mt)
            return carry

        lax.fori_loop(0, nkb, max_body, 0)
        for h in range(ATT_HEADS):
            m_ref[h] = jnp.broadcast_to(jnp.max(m_ref[h], axis=1, keepdims=True), (tq, LANES))
        exp_and_values()

    for h in range(ATT_HEADS):
        acc = acc_ref[h]
        o_ref[:, h * HEAD_DIM:(h + 1) * HEAD_DIM] = (
            acc[:, :HEAD_DIM] / acc[:, HEAD_DIM:HEAD_DIM + 1]).astype(BF16)


def _attn_call(q, qi, proj, k, v, ki, *, row0, nb, length, s_pad, s_valid, q_off, tq, kb, topk):
    nq = length // tq
    rb0 = row0 // tq
    nkb_max = s_pad // kb
    bpg = 32 // (kb // LANES)
    ng_max = -(-nkb_max // bpg)
    kern = functools.partial(_attn_kernel, tq=tq, kb=kb, s_valid=s_valid, q_off=q_off, topk=topk)
    return pl.pallas_call(
        kern,
        grid=(nb, nq),
        in_specs=[pl.BlockSpec((tq, 512), lambda b, i: (b * nq + i, 0)),
                  pl.BlockSpec((tq, 256), lambda b, i: (b * nq + i, 0)),
                  pl.BlockSpec((tq, 128), lambda b, i: (rb0 + b * nq + i, COL_KIWI // 128)),
                  pl.BlockSpec((s_pad, 128), lambda b, i: (b, 0)),
                  pl.BlockSpec((s_pad, 2 * LANES), lambda b, i: (b, 0)),
                  pl.BlockSpec((s_pad, IDX_DIM), lambda b, i: (b, 0))],
        out_specs=pl.BlockSpec((tq, 512), lambda b, i: (b * nq + i, 0)),
        out_shape=jax.ShapeDtypeStruct((nb * length, 512), BF16),
        scratch_shapes=[pltpu.VMEM((ng_max * bpg, tq, kb), I32), pltpu.VMEM((nkb_max, tq, kb), F32),
                        pltpu.VMEM((ng_max * 32, tq, LANES), I32), pltpu.VMEM((ng_max, tq, LANES), I32),
                        pltpu.VMEM((ATT_HEADS, tq, LANES), F32), pltpu.VMEM((ATT_HEADS, tq, LANES), F32),
                        pltpu.VMEM((nkb_max, 8, LANES), F32)],
        compiler_params=_cparams(("parallel", "arbitrary")),
        name="dsa_attention",
    )(q, qi, proj, k, v, ki)


def _ssm_kernel(z_ref, xs_ref, dt_ref, bc_ref, cx0_ref, cbc0_ref, s0_ref,
                cwx_ref, cwbc_ref, cbx_ref, cbbc_ref, dtb_ref, alog_ref, dsk_ref, nw_ref,
                y_ref, sT_ref, cxT_ref, cbcT_ref,
                xpx_ref, xpbc_ref, xc_ref, bcc_ref, dtc_ref, ypre_ref, st_ref,
                *, t_in, t_pad):
    j = pl.program_id(1)
    nj = pl.num_programs(1)
    q = CHUNK

    @pl.when(j == 0)
    def _():
        xpx_ref[0:8, :] = cx0_ref[...]
        xpbc_ref[0:8, :] = cbc0_ref[...]
        st_ref[...] = s0_ref[...]

    xpx_ref[8:8 + t_in, :] = xs_ref[...]
    xpbc_ref[8:8 + t_in, :] = bc_ref[...]

    def conv(xp_ref, w_ref, b_ref):
        y = b_ref[...]
        for tap in range(4):
            y = y + xp_ref[5 + tap:5 + tap + t_in, :] * w_ref[tap:tap + 1, :]
        return _silu(y)

    if t_pad > t_in:
        xc_ref[...] = jnp.zeros_like(xc_ref)
        bcc_ref[...] = jnp.zeros_like(bcc_ref)
        dtc_ref[...] = jnp.zeros_like(dtc_ref)
    xc_ref[0:t_in, :] = conv(xpx_ref, cwx_ref, cbx_ref)
    bcc_ref[0:t_in, :] = conv(xpbc_ref, cwbc_ref, cbbc_ref)
    dtc_ref[0:t_in, :] = _softplus(dt_ref[...] + dtb_ref[...])

    last_x = xpx_ref[t_in:t_in + 8, :]
    last_bc = xpbc_ref[t_in:t_in + 8, :]
    xpx_ref[0:8, :] = last_x
    xpbc_ref[0:8, :] = last_bc

    a_neg = -jnp.exp(alog_ref[...])
    li = lax.broadcasted_iota(I32, (q, q), 0)
    si = lax.broadcasted_iota(I32, (q, q), 1)
    tri = jnp.where(si <= li, 1.0, 0.0).astype(BF16)
    ones = jnp.ones((q, q), BF16)
    lane = lax.broadcasted_iota(I32, (q, MIX), 1)
    row = lax.broadcasted_iota(I32, (q, MIX), 0)
    s_of_lane = lane % q
    mask_t_le_s = jnp.where(row <= s_of_lane, 1.0, 0.0)
    causal = s_of_lane <= row
    rg = lax.broadcasted_iota(I32, (SSM_HEADS * q, 2 * SSM_STATE), 0) // (q * SSM_HEADS // SSM_GROUPS)
    cg = lax.broadcasted_iota(I32, (SSM_HEADS * q, 2 * SSM_STATE), 1) // SSM_STATE
    gmask = rg == cg
    rh = lax.broadcasted_iota(I32, (SSM_HEADS * q, MIX), 0) // q
    ch = lax.broadcasted_iota(I32, (SSM_HEADS * q, MIX), 1) // SSM_HEAD_DIM
    hmask = rh == ch
    r2 = lax.broadcasted_iota(I32, (2 * SSM_STATE, MIX), 0) // SSM_STATE
    c2 = lax.broadcasted_iota(I32, (2 * SSM_STATE, MIX), 1) // (MIX // SSM_GROUPS)
    g2mask = r2 == c2
    nt_dims = (((1,), (1,)), ((), ()))
    tn_dims = (((0,), (0,)), ((), ()))

    def sum_rows(sel01, a):
        a_hi = a.astype(BF16)
        r1 = a - a_hi.astype(F32)
        a_mid = r1.astype(BF16)
        a_lo = (r1 - a_mid.astype(F32)).astype(BF16)
        return (jnp.dot(sel01, a_hi, preferred_element_type=F32)
                + jnp.dot(sel01, a_mid, preferred_element_type=F32)
                + jnp.dot(sel01, a_lo, preferred_element_type=F32))

    def chunk_body(c, carry):
        r0 = pl.multiple_of(c * q, q)
        xs = xc_ref[pl.ds(r0, q), :]
        dt = dtc_ref[pl.ds(r0, q), :]
        bmat = bcc_ref[pl.ds(r0, q), 0:2 * SSM_STATE]
        cmat = bcc_ref[pl.ds(r0, q), 2 * SSM_STATE:4 * SSM_STATE]
        a = dt * a_neg
        xdt = xs * dt
        acum = sum_rows(tri, a)
        rowt = sum_rows(ones, a * mask_t_le_s)
        decay_in = jnp.where(causal, jnp.exp(acum - rowt), 0.0)
        bexp = jnp.where(gmask, jnp.concatenate([bmat] * SSM_HEADS, axis=0), 0.0)
        cb = lax.dot_general(cmat.astype(BF16), bexp.astype(BF16), nt_dims,
                             preferred_element_type=F32)
        m = (cb * decay_in).astype(BF16)
        bdx = jnp.where(hmask, jnp.concatenate([xdt] * SSM_HEADS, axis=0), 0.0).astype(BF16)
        y_diag = jnp.dot(m, bdx, preferred_element_type=F32)
        st = st_ref[...]
        y_off = jnp.exp(acum) * jnp.dot(cmat.astype(BF16), st.astype(BF16),
                                        preferred_element_type=F32)
        a_end = acum[q - 1:q, :]
        xd = (xdt * jnp.exp(a_end - acum)).astype(BF16)
        upd = lax.dot_general(bmat.astype(BF16), xd, tn_dims, preferred_element_type=F32)
        st_ref[...] = jnp.exp(a_end) * st + jnp.where(g2mask, upd, 0.0)
        ypre_ref[pl.ds(r0, q), :] = y_diag + y_off
        return carry

    lax.fori_loop(0, t_pad // q, chunk_body, 0)

    xs = xc_ref[0:t_in, :]
    y = ypre_ref[0:t_in, :] + dsk_ref[...] * xs
    y = y * _silu(z_ref[...])
    half = MIX // SSM_GROUPS
    parts = []
    for g in range(SSM_GROUPS):
        yg = y[:, g * half:(g + 1) * half]
        parts.append(yg * lax.rsqrt(jnp.mean(yg * yg, axis=-1, keepdims=True) + NORM_EPS))
    y = jnp.concatenate(parts, axis=1) * nw_ref[...]
    y_ref[...] = y.astype(BF16)

    @pl.when(j == nj - 1)
    def _():
        sT_ref[...] = st_ref[...]
        cxT_ref[...] = last_x
        cbcT_ref[...] = last_bc


def _ssm_call(proj, cx0, cbc0, s0, wts, *, row0, nb, length, tl):
    nt = length // tl
    rb0 = row0 // tl
    t_pad = -(-tl // CHUNK) * CHUNK
    kern = functools.partial(_ssm_kernel, t_in=tl, t_pad=t_pad)

    def pspec(width, col):
        return pl.BlockSpec((tl, width), lambda b, j: (rb0 + b * nt + j, col // width))

    def bspec(r, c):
        return pl.BlockSpec((None, r, c), lambda b, j: (b, 0, 0))

    def wspec(r, c):
        return pl.BlockSpec((r, c), lambda b, j: (0, 0))

    return pl.pallas_call(
        kern,
        grid=(nb, nt),
        in_specs=[pspec(512, COL_Z), pspec(512, COL_XS), pspec(512, COL_DT), pspec(256, COL_BC),
                  bspec(8, MIX), bspec(8, SSM_BC), bspec(2 * SSM_STATE, MIX),
                  wspec(4, MIX), wspec(4, SSM_BC), wspec(1, MIX), wspec(1, SSM_BC),
                  wspec(1, MIX), wspec(1, MIX), wspec(1, MIX), wspec(1, MIX)],
        out_specs=[pl.BlockSpec((tl, MIX), lambda b, j: (b * nt + j, 0)),
                   bspec(2 * SSM_STATE, MIX), bspec(8, MIX), bspec(8, SSM_BC)],
        out_shape=[jax.ShapeDtypeStruct((nb * length, MIX), BF16),
                   jax.ShapeDtypeStruct((nb, 2 * SSM_STATE, MIX), F32),
                   jax.ShapeDtypeStruct((nb, 8, MIX), F32),
                   jax.ShapeDtypeStruct((nb, 8, SSM_BC), F32)],
        scratch_shapes=[pltpu.VMEM((tl + 8, MIX), F32), pltpu.VMEM((tl + 8, SSM_BC), F32),
                        pltpu.VMEM((t_pad, MIX), F32), pltpu.VMEM((t_pad, SSM_BC), F32),
                        pltpu.VMEM((t_pad, MIX), F32), pltpu.VMEM((t_pad, MIX), F32),
                        pltpu.VMEM((2 * SSM_STATE, MIX), F32)],
        compiler_params=_cparams(("parallel", "arbitrary")),
        name="ssd_branch",
    )(proj, proj, proj, proj, cx0, cbc0, s0, *wts)


def _lru_kernel(xl_ref, gl_ref, c0_ref, h0_ref, cw_ref, cb_ref, wa_ref, ba_ref, wx_ref, bx_ref,
                lam_ref, y_ref, hT_ref, cT_ref, xp_ref, a_ref, u_ref, hs_ref, h_ref, *, tl):
    j = pl.program_id(1)
    nj = pl.num_programs(1)

    @pl.when(j == 0)
    def _():
        xp_ref[0:8, :] = c0_ref[...]
        h_ref[...] = h0_ref[...]

    xp_ref[8:8 + tl, :] = xl_ref[...]
    xc = cb_ref[...]
    for tap in range(4):
        xc = xc + xp_ref[5 + tap:5 + tap + tl, :] * cw_ref[tap:tap + 1, :]
    last = xp_ref[tl:tl + 8, :]
    xp_ref[0:8, :] = last

    xcb = xc.astype(BF16)
    r = jax.nn.sigmoid(jnp.dot(xcb, wa_ref[...], preferred_element_type=F32) + ba_ref[...])
    i = jax.nn.sigmoid(jnp.dot(xcb, wx_ref[...], preferred_element_type=F32) + bx_ref[...])
    log_a = (-LRU_C * _softplus(-lam_ref[...])) * r
    a = jnp.exp(log_a)
    mult = jnp.sqrt(-jnp.tanh(log_a) * (a * a + 1.0))
    a_ref[...] = a
    u_ref[...] = mult * (i * xc)

    def step(t, h):
        h = a_ref[pl.ds(t, 1), :] * h + u_ref[pl.ds(t, 1), :]
        hs_ref[pl.ds(t, 1), :] = h
        return h

    h = lax.fori_loop(0, tl, step, h_ref[...], unroll=8)
    h_ref[...] = h
    y_ref[...] = (hs_ref[...] * jax.nn.gelu(gl_ref[...], approximate=True)).astype(BF16)

    @pl.when(j == nj - 1)
    def _():
        hT_ref[...] = h
        cT_ref[...] = last


def _lru_call(proj, c0, h0, wts, *, row0, nb, length, tl):
    nt = length // tl
    rb0 = row0 // tl
    kern = functools.partial(_lru_kernel, tl=tl)

    def pspec(col):
        return pl.BlockSpec((tl, MIX), lambda b, j: (rb0 + b * nt + j, col // MIX))

    def bspec(r):
        return pl.BlockSpec((None, r, MIX), lambda b, j: (b, 0, 0))

    def wspec(r):
        return pl.BlockSpec((r, MIX), lambda b, j: (0, 0))

    return pl.pallas_call(
        kern,
        grid=(nb, nt),
        in_specs=[pspec(COL_XL), pspec(COL_GL), bspec(8), bspec(1),
                  wspec(4), wspec(1), wspec(MIX), wspec(1), wspec(MIX), wspec(1), wspec(1)],
        out_specs=[pl.BlockSpec((tl, MIX), lambda b, j: (b * nt + j, 0)), bspec(1), bspec(8)],
        out_shape=[jax.ShapeDtypeStruct((nb * length, MIX), BF16),
                   jax.ShapeDtypeStruct((nb, 1, MIX), F32),
                   jax.ShapeDtypeStruct((nb, 8, MIX), F32)],
        scratch_shapes=[pltpu.VMEM((tl + 8, MIX), F32), pltpu.VMEM((tl, MIX), F32),
                        pltpu.VMEM((tl, MIX), F32), pltpu.VMEM((tl, MIX), F32),
                        pltpu.VMEM((1, MIX), F32)],
        compiler_params=_cparams(("parallel", "arbitrary")),
        name="rglru_branch",
    )(proj, proj, c0, h0, *wts)


def _merge_kernel(x_ref, att_ref, ssm_ref, lru_ref, g0_ref, g1_ref, g2_ref,
                  wa_ref, ws_ref, wl_ref, wo_ref, o_ref):
    merged = (jax.nn.sigmoid(g0_ref[...]) * jnp.dot(att_ref[...], wa_ref[...], preferred_element_type=F32)
              + jax.nn.sigmoid(g1_ref[...]) * jnp.dot(ssm_ref[...], ws_ref[...], preferred_element_type=F32)
              + jax.nn.sigmoid(g2_ref[...]) * jnp.dot(lru_ref[...], wl_ref[...], preferred_element_type=F32))
    o_ref[...] = x_ref[...] + jnp.dot(merged.astype(BF16), wo_ref[...], preferred_element_type=F32)


def _merge_call(x, att, ssm, lru, proj, wa, ws, wl, wo, *, tm):
    n, d = x.shape

    def rspec(width):
        return pl.BlockSpec((tm, width), lambda i: (i, 0))

    def gspec(k):
        return pl.BlockSpec((tm, d), lambda i: (i, COL_GATES // d + k))

    def wspec(r):
        return pl.BlockSpec((r, d), lambda i: (0, 0))

    return pl.pallas_call(
        _merge_kernel,
        grid=(n // tm,),
        in_specs=[rspec(d), rspec(MIX), rspec(MIX), rspec(MIX), gspec(0), gspec(1), gspec(2),
                  wspec(MIX), wspec(MIX), wspec(MIX), wspec(d)],
        out_specs=rspec(d),
        out_shape=jax.ShapeDtypeStruct((n, d), F32),
        compiler_params=_cparams(("parallel",)),
        name="branch_merge",
    )(x, att, ssm, lru, proj, proj, proj, wa, ws, wl, wo)


def _ffn_kernel(x_ref, g_ref, w1_ref, w3_ref, w2_ref, o_ref, xn_ref, acc_ref):
    f = pl.program_id(1)

    @pl.when(f == 0)
    def _():
        xn_ref[...] = _rms(x_ref[...], g_ref[...]).astype(BF16)
        acc_ref[...] = jnp.zeros_like(acc_ref)

    xn = xn_ref[...]
    h1 = jnp.dot(xn, w1_ref[...], preferred_element_type=F32)
    h3 = jnp.dot(xn, w3_ref[...], preferred_element_type=F32)
    h = (_silu(h1) * h3).astype(BF16)
    acc_ref[...] += jnp.dot(h, w2_ref[...], preferred_element_type=F32)

    @pl.when(f == pl.num_programs(1) - 1)
    def _():
        o_ref[...] = x_ref[...] + acc_ref[...]


def _ffn_call(x, g, w1, w3, w2, *, tm, tf):
    n, d = x.shape
    dff = w1.shape[1]
    return pl.pallas_call(
        _ffn_kernel,
        grid=(n // tm, dff // tf),
        in_specs=[pl.BlockSpec((tm, d), lambda i, f: (i, 0)),
                  pl.BlockSpec((1, d), lambda i, f: (0, 0)),
                  pl.BlockSpec((d, tf), lambda i, f: (0, f)),
                  pl.BlockSpec((d, tf), lambda i, f: (0, f)),
                  pl.BlockSpec((tf, d), lambda i, f: (f, 0))],
        out_specs=pl.BlockSpec((tm, d), lambda i, f: (i, 0)),
        out_shape=jax.ShapeDtypeStruct((n, d), F32),
        scratch_shapes=[pltpu.VMEM((tm, d), BF16), pltpu.VMEM((tm, d), F32)],
        compiler_params=_cparams(("parallel", "arbitrary")),
        name="swiglu_ffn",
    )(x, g, w1, w3, w2)


def _router_kernel(x_ref, g_ref, wr_ref, gate_ref, xn_ref):
    xn = _rms(x_ref[...], g_ref[...])
    xn_ref[...] = xn.astype(BF16)
    logits = jnp.dot(xn, wr_ref[...], precision=lax.Precision.HIGHEST, preferred_element_type=F32)
    lane = lax.broadcasted_iota(I32, logits.shape, 1)
    logits = jnp.where(lane < N_EXPERTS, logits, -jnp.inf)
    m1 = jnp.max(logits, axis=1, keepdims=True)
    i1 = jnp.min(jnp.where(logits == m1, lane, LANES), axis=1, keepdims=True)
    rest = jnp.where(lane == i1, -jnp.inf, logits)
    m2 = jnp.max(rest, axis=1, keepdims=True)
    i2 = jnp.min(jnp.where(rest == m2, lane, LANES), axis=1, keepdims=True)
    e2 = jnp.exp(m2 - m1)
    den = 1.0 + e2
    gate_ref[...] = jnp.where(lane == i1, 1.0 / den, 0.0) + jnp.where(lane == i2, e2 / den, 0.0)


def _router_call(x, g, wr, *, tm):
    n, d = x.shape
    return pl.pallas_call(
        _router_kernel,
        grid=(n // tm,),
        in_specs=[pl.BlockSpec((tm, d), lambda i: (i, 0)),
                  pl.BlockSpec((1, d), lambda i: (0, 0)),
                  pl.BlockSpec((d, LANES), lambda i: (0, 0))],
        out_specs=[pl.BlockSpec((tm, LANES), lambda i: (i, 0)), pl.BlockSpec((tm, d), lambda i: (i, 0))],
        out_shape=[jax.ShapeDtypeStruct((n, LANES), F32), jax.ShapeDtypeStruct((n, d), BF16)],
        compiler_params=_cparams(("parallel",)),
        name="moe_router",
    )(x, g, wr)


def _moe_expert_kernel(nch_ref, slot_ref, xn_ref, w1_ref, w3_ref, w2_ref, yc_ref, xg_ref, yacc_ref,
                       *, ch):
    e = pl.program_id(0)
    t = pl.program_id(1)
    f = pl.program_id(2)
    nch = nch_ref[t * pl.num_programs(0) + e]

    @pl.when(f == 0)
    def _():
        srow = slot_ref[pl.ds(e, 1), :]

        def gather(c, carry):
            rows = c * ch + lax.broadcasted_iota(I32, (ch, 1), 0)
            onehot = jnp.where(srow == rows, 1.0, 0.0).astype(BF16)
            xg_ref[c] = jnp.dot(onehot, xn_ref[...], preferred_element_type=F32).astype(BF16)
            yacc_ref[c] = jnp.zeros((ch, yacc_ref.shape[2]), F32)
            return carry

        lax.fori_loop(0, nch, gather, 0)

    def ffn(c, carry):
        xg = xg_ref[c]
        h1 = jnp.dot(xg, w1_ref[...], preferred_element_type=F32)
        h3 = jnp.dot(xg, w3_ref[...], preferred_element_type=F32)
        h = (_silu(h1) * h3).astype(BF16)
        yacc_ref[c] += jnp.dot(h, w2_ref[...], preferred_element_type=F32)
        return carry

    lax.fori_loop(0, nch, ffn, 0)

    @pl.when(f == pl.num_programs(2) - 1)
    def _():
        def emit(c, carry):
            yc_ref[pl.ds(pl.multiple_of(c * ch, ch), ch), :] = yacc_ref[c].astype(BF16)
            return carry

        def clear(c, carry):
            yc_ref[pl.ds(pl.multiple_of(c * ch, ch), ch), :] = jnp.zeros((ch, yc_ref.shape[1]), BF16)
            return carry

        lax.fori_loop(0, nch, emit, 0)
        lax.fori_loop(nch, yc_ref.shape[0] // ch, clear, 0)


def _moe_expert_call(nch, slot_exp, xn, w1, w3, w2, *, tm, tf, ch):
    n, d = xn.shape
    ne, _, dff = w1.shape
    nt = n // tm
    nch_max = -(-tm // ch)
    kern = functools.partial(_moe_expert_kernel, ch=ch)
    return pl.pallas_call(
        kern,
        grid_spec=pltpu.PrefetchScalarGridSpec(
            num_scalar_prefetch=1,
            grid=(ne, nt, dff // tf),
            in_specs=[pl.BlockSpec((None, ne, tm), lambda e, t, f, nch: (t, 0, 0)),
                      pl.BlockSpec((tm, d), lambda e, t, f, nch: (t, 0)),
                      pl.BlockSpec((None, d, tf), lambda e, t, f, nch: (e, 0, f)),
                      pl.BlockSpec((None, d, tf), lambda e, t, f, nch: (e, 0, f)),
                      pl.BlockSpec((None, tf, d), lambda e, t, f, nch: (e, f, 0))],
            out_specs=pl.BlockSpec((None, None, nch_max * ch, d), lambda e, t, f, nch: (e, t, 0, 0)),
            scratch_shapes=[pltpu.VMEM((nch_max, ch, d), BF16), pltpu.VMEM((nch_max, ch, d), F32)]),
        out_shape=jax.ShapeDtypeStruct((ne, nt, nch_max * ch, d), BF16),
        compiler_params=_cparams(("parallel", "parallel", "arbitrary")),
        name="moe_experts",
    )(nch, slot_exp, xn, w1, w3, w2)


def _moe_combine_kernel(nch_ref, x_ref, slot_ref, gate_ref, yc_ref, g_ref, o_ref, *, ch, final_norm):
    t = pl.program_id(0)
    e = pl.program_id(1)
    nch = nch_ref[t * pl.num_programs(1) + e]

    @pl.when(e == 0)
    def _():
        o_ref[...] = x_ref[...]

    lane = lax.broadcasted_iota(I32, slot_ref.shape, 1)
    scol = jnp.sum(jnp.where(lane == e, slot_ref[...], 0.0), axis=1, keepdims=True)
    gcol = jnp.sum(jnp.where(lane == e, gate_ref[...], 0.0), axis=1, keepdims=True)

    def scatter(c, carry):
        cols = (c * ch + lax.broadcasted_iota(I32, (1, ch), 1)).astype(F32)
        onehot = jnp.where(scol == cols, 1.0, 0.0).astype(BF16)
        yc = yc_ref[pl.ds(pl.multiple_of(c * ch, ch), ch), :]
        o_ref[...] += gcol * jnp.dot(onehot, yc, preferred_element_type=F32)
        return carry

    lax.fori_loop(0, nch, scatter, 0)

    if final_norm:
        @pl.when(e == pl.num_programs(1) - 1)
        def _():
            o_ref[...] = _rms(o_ref[...], g_ref[...])


def _moe_combine_call(nch, x, slot_tok, gate, yc, g_final, *, tm, ch, final_norm):
    n, d = x.shape
    ne, nt, rows, _ = yc.shape
    kern = functools.partial(_moe_combine_kernel, ch=ch, final_norm=final_norm)
    return pl.pallas_call(
        kern,
        grid_spec=pltpu.PrefetchScalarGridSpec(
            num_scalar_prefetch=1,
            grid=(nt, ne),
            in_specs=[pl.BlockSpec((tm, d), lambda t, e, nch: (t, 0)),
                      pl.BlockSpec((tm, LANES), lambda t, e, nch: (t, 0)),
                      pl.BlockSpec((tm, LANES), lambda t, e, nch: (t, 0)),
                      pl.BlockSpec((None, None, rows, d), lambda t, e, nch: (e, t, 0, 0)),
                      pl.BlockSpec((1, d), lambda t, e, nch: (0, 0))],
            out_specs=pl.BlockSpec((tm, d), lambda t, e, nch: (t, 0))),
        out_shape=jax.ShapeDtypeStruct((n, d), F32),
        compiler_params=_cparams(("parallel", "arbitrary")),
        name="moe_combine",
    )(nch, x, slot_tok, gate, yc, g_final)


def _moe_routing_tables(gate, *, tm, ch):
    n = gate.shape[0]
    nt = n // tm
    routed = (gate[:, :N_EXPERTS] != 0.0).astype(I32).reshape(nt, tm, N_EXPERTS)
    rank = jnp.cumsum(routed, axis=1) - routed
    slot = jnp.where(routed > 0, rank, -1)
    counts = jnp.sum(routed, axis=1)
    nch = ((counts + ch - 1) // ch).reshape(nt * N_EXPERTS).astype(I32)
    slot_exp = slot.transpose(0, 2, 1)
    slot_tok = jnp.pad(slot.reshape(n, N_EXPERTS).astype(F32), ((0, 0), (0, LANES - N_EXPERTS)),
                       constant_values=-1.0)
    return nch, slot_exp, slot_tok


def _norm_kernel(x_ref, g_ref, o_ref):
    o_ref[...] = _rms(x_ref[...], g_ref[...])


def _norm_call(x, g, *, tm):
    n, d = x.shape
    return pl.pallas_call(
        _norm_kernel,
        grid=(n // tm,),
        in_specs=[pl.BlockSpec((tm, d), lambda i: (i, 0)), pl.BlockSpec((1, d), lambda i: (0, 0))],
        out_specs=pl.BlockSpec((tm, d), lambda i: (i, 0)),
        out_shape=jax.ShapeDtypeStruct((n, d), F32),
        compiler_params=_cparams(("parallel",)),
        name="final_norm",
    )(x, g)


def _pack_w_in(w):
    d = w.shape[0]
    o = np.cumsum([0, 512, 128, 128, 256, 64, 4, 512, 768, 8, 512, 512, 3072])
    seg = lambda k: w[:, int(o[k]):int(o[k + 1])]
    q, k, v, qi, ki, wi, z, xbc, dt, xl, gl, gates = [seg(t) for t in range(12)]
    xs, bc = xbc[:, :MIX], xbc[:, MIX:]
    dt_exp = jnp.repeat(dt, SSM_HEAD_DIM, axis=1)
    pieces = ((COL_Q, q * (HEAD_DIM ** -0.5 * LOG2_E)), (COL_Z, z), (COL_XL, xl), (COL_GL, gl),
              (COL_GATES, gates), (COL_XS, xs), (COL_DT, dt_exp), (COL_BC, bc), (COL_QI, qi),
              (COL_K, k), (COL_V, v), (COL_KIWI, ki), (COL_KIWI + IDX_DIM, wi))
    packed = jnp.zeros((d, PROJ_W), BF16)
    for col, piece in pieces:
        packed = lax.dynamic_update_slice(packed, piece.astype(BF16), (0, col))
    return packed


def _block_diag(w):
    nblk, bw, _ = w.shape
    eye = jnp.eye(nblk, dtype=w.dtype)
    return jnp.einsum('kij,kl->kilj', w, eye).reshape(nblk * bw, nblk * bw)


def _rope_tables(pos):
    half = HEAD_DIM // 2
    inv = 1.0 / (ROPE_THETA ** (jnp.arange(half, dtype=F32) / half))
    ang = pos.astype(F32)[:, None] * inv[None, :]
    cos, sin = jnp.cos(ang), jnp.sin(ang)
    cos_t = jnp.concatenate([cos, cos, cos, cos], axis=1)
    sin_t = jnp.concatenate([-sin, sin, -sin, sin], axis=1)
    return cos_t, sin_t


def _pad_rows8(a):
    return jnp.pad(a, ((0, 0), (5, 0), (0, 0)))


def _state_to_s2(h):
    nb = h.shape[0]
    hg = h.reshape(nb, SSM_GROUPS, SSM_HEADS // SSM_GROUPS, SSM_HEAD_DIM, SSM_STATE)
    eye = jnp.eye(SSM_GROUPS, dtype=h.dtype)
    s2 = jnp.einsum('bgkpn,gf->bfngkp', hg, eye)
    return s2.reshape(nb, SSM_GROUPS * SSM_STATE, MIX)


def _s2_to_state(s2):
    nb = s2.shape[0]
    s6 = s2.reshape(nb, SSM_GROUPS, SSM_STATE, SSM_GROUPS, SSM_HEADS // SSM_GROUPS, SSM_HEAD_DIM)
    diag = jnp.stack([s6[:, g, :, g] for g in range(SSM_GROUPS)], axis=1)
    return diag.transpose(0, 1, 3, 4, 2).reshape(nb, SSM_HEADS, SSM_HEAD_DIM, SSM_STATE)


def _with_ones_column(v):
    n = v.shape[0]
    e = jnp.zeros((n, LANES - HEAD_DIM), v.dtype).at[:, 0].set(1)
    return jnp.concatenate([v[:, :HEAD_DIM], e, v[:, HEAD_DIM:], e], axis=1)


def _expand_heads(v):
    return jnp.repeat(v, SSM_HEAD_DIM)[None, :]


def kernel(x_prompt, x_sample, cache_k, cache_v, cache_kidx, state_ssm, state_ssm_conv, state_lru,
           state_lru_conv, norm_mix, norm_ffn, norm_final, w_in, ssm_conv_w, ssm_conv_b, ssm_dt_bias,
           ssm_a_log, ssm_d, ssm_norm, lru_conv_w, lru_conv_b, lru_wa, lru_ba, lru_wx, lru_bx, lru_lambda,
           w_att_out, w_ssm_out, w_lru_out, w_o, ffn_w1, ffn_w3, ffn_w2, moe_router, moe_w1, moe_w3, moe_w2):
    pb, pl_len, d = x_prompt.shape
    sb, sl_len, _ = x_sample.shape
    depth = w_in.shape[0]
    past = cache_k.shape[2]
    n_p = pb * pl_len
    n_s = sb * sl_len
    x = jnp.concatenate([x_prompt.reshape(n_p, d), x_sample.reshape(n_s, d)], axis=0)

    topk_p = min(TOPK_MAX, pl_len // 4)
    s_tot = past + sl_len
    topk_s = min(TOPK_MAX, s_tot // 4)
    kb_s = 512
    s_pad = -(-s_tot // kb_s) * kb_s

    cos_p, sin_p = _rope_tables(jnp.arange(pl_len))
    cos_s, sin_s = _rope_tables(past + jnp.arange(sl_len))

    groups = (
        dict(row0=0, nb=pb, length=pl_len),
        dict(row0=n_p, nb=sb, length=sl_len),
    )
    tl_p, tl_s = 256, sl_len

    collected = [[[] for _ in range(7)] for _ in range(2)]
    final_fused = False
    for layer in range(depth):
        proj = _norm_matmul(x, norm_mix[layer][None, :], _pack_w_in(w_in[layer]), tm=1536, tn=1024)

        ssm_w = (ssm_conv_w[layer][:, :MIX], ssm_conv_w[layer][:, MIX:],
                 ssm_conv_b[layer][None, :MIX], ssm_conv_b[layer][None, MIX:],
                 _expand_heads(ssm_dt_bias[layer]), _expand_heads(ssm_a_log[layer]),
                 _expand_heads(ssm_d[layer]), ssm_norm[layer][None, :])
        lru_w = (lru_conv_w[layer], lru_conv_b[layer][None, :],
                 _block_diag(lru_wa[layer]).astype(BF16), lru_ba[layer][None, :],
                 _block_diag(lru_wx[layer]).astype(BF16), lru_bx[layer][None, :],
                 lru_lambda[layer][None, :])

        branch = [[], [], []]
        for gi, grp in enumerate(groups):
            nb, length = grp['nb'], grp['length']
            if gi == 0:
                tl, cos, sin = tl_p, cos_p, sin_p
                cx0 = jnp.zeros((nb, 8, MIX), F32)
                cbc0 = jnp.zeros((nb, 8, SSM_BC), F32)
                s0 = jnp.zeros((nb, 2 * SSM_STATE, MIX), F32)
                lc0 = jnp.zeros((nb, 8, MIX), F32)
                lh0 = jnp.zeros((nb, 1, MIX), F32)
            else:
                tl, cos, sin = tl_s, cos_s, sin_s
                conv0 = _pad_rows8(state_ssm_conv[layer])
                cx0, cbc0 = conv0[:, :, :MIX], conv0[:, :, MIX:]
                s0 = _state_to_s2(state_ssm[layer])
                lc0 = _pad_rows8(state_lru_conv[layer])
                lh0 = state_lru[layer][:, None, :]

            q_r, qi_r, k_r, v_r, ki_r, k_b, v_b, ki_b = _rope_call(proj, cos, sin, tl=tl, **grp)
            if gi == 0:
                att = _attn_call(q_r, qi_r, proj, k_b, _with_ones_column(v_b), ki_b, s_pad=length,
                                 s_valid=length, q_off=0, tq=256, kb=512, topk=topk_p, **grp)
            else:
                def cat(cache, new, width):
                    c = cache.reshape(nb, past, width).astype(BF16)
                    a = jnp.concatenate([c, new.reshape(nb, length, width)], axis=1)
                    a = jnp.pad(a, ((0, 0), (0, s_pad - s_tot), (0, 0)))
                    return a.reshape(nb * s_pad, width)
                att = _attn_call(q_r, qi_r, proj, cat(cache_k[layer], k_b, 128),
                                 _with_ones_column(cat(cache_v[layer], v_b, 128)),
                                 cat(cache_kidx[layer], ki_b, IDX_DIM), s_pad=s_pad, s_valid=s_tot,
                                 q_off=past, tq=length, kb=kb_s, topk=topk_s, **grp)
            y_ssm, s_t, cx_t, cbc_t = _ssm_call(proj, cx0, cbc0, s0, ssm_w, tl=tl, **grp)
            y_lru, lh_t, lc_t = _lru_call(proj, lc0, lh0, lru_w, tl=tl, **grp)
            branch[0].append(att)
            branch[1].append(y_ssm)
            branch[2].append(y_lru)

            st = (k_r.reshape(nb, length, KV_HEADS, HEAD_DIM), v_r.reshape(nb, length, KV_HEADS, HEAD_DIM),
                  ki_r.reshape(nb, length, IDX_DIM), _s2_to_state(s_t),
                  jnp.concatenate([cx_t[:, 5:], cbc_t[:, 5:]], axis=2), lh_t[:, 0], lc_t[:, 5:])
            for lst, s in zip(collected[gi], st):
                lst.append(s)

        att, y_ssm, y_lru = [jnp.concatenate(bl, axis=0) for bl in branch]
        x = _merge_call(x, att, y_ssm, y_lru, proj, w_att_out[layer].astype(BF16),
                        w_ssm_out[layer].astype(BF16), w_lru_out[layer].astype(BF16),
                        w_o[layer].astype(BF16), tm=512)
        jl = layer // 2
        gf = norm_ffn[layer][None, :]
        if layer % 2 == 0:
            x = _ffn_call(x, gf, ffn_w1[jl].astype(BF16), ffn_w3[jl].astype(BF16),
                          ffn_w2[jl].astype(BF16), tm=768, tf=1408)
        else:
            wr = jnp.pad(moe_router[jl], ((0, 0), (0, LANES - N_EXPERTS)))
            gate, xn = _router_call(x, gf, wr, tm=512)
            nch, slot_exp, slot_tok = _moe_routing_tables(gate, tm=MOE_TM, ch=MOE_CH)
            yc = _moe_expert_call(nch, slot_exp, xn, moe_w1[jl].astype(BF16), moe_w3[jl].astype(BF16),
                                  moe_w2[jl].astype(BF16), tm=MOE_TM, tf=1408, ch=MOE_CH)
            final_fused = layer == depth - 1
            x = _moe_combine_call(nch, x, slot_tok, gate, yc, norm_final[None, :], tm=MOE_TM,
                                  ch=MOE_CH, final_norm=final_fused)

    y = x if final_fused else _norm_call(x, norm_final[None, :], tm=512)
    y_prompt = y[:n_p].reshape(pb, pl_len, d)
    y_sample = y[n_p:].reshape(sb, sl_len, d)
    p_states = [jnp.stack(lst, axis=0) for lst in collected[0]]
    s_states = [jnp.stack(lst, axis=0) for lst in collected[1]]
    return (y_prompt, y_sample, *p_states, *s_states)
```

```python
import functools

import jax
import jax.numpy as jnp
import numpy as np
from jax import lax
from jax.experimental import pallas as pl
from jax.experimental.pallas import tpu as pltpu

F32 = jnp.float32
BF16 = jnp.bfloat16
I32 = jnp.int32

CHUNK = 64
HEAD_DIM = 64
ATT_HEADS = 8
KV_HEADS = 2
IDX_HEADS = 4
IDX_DIM = 64
TOPK_MAX = 256
ROPE_THETA = 10000.0
MIX = 512
SSM_HEADS = 8
SSM_HEAD_DIM = 64
SSM_GROUPS = 2
SSM_STATE = 64
SSM_BC = 2 * SSM_GROUPS * SSM_STATE
LRU_C = 8.0
N_EXPERTS = 8
NORM_EPS = 1e-6

LANES = 128
VMEM_LIMIT = 56 * 1024 * 1024

IN_PROJ_TM, IN_PROJ_TN = 1536, 1024
MERGE_TM = 512
FFN_TM, FFN_TF = 768, 1408
ROW_TM = 512
SEQ_TL = 256
ATT_TQ, ATT_KB = 256, 512
MOE_TM = 1536
MOE_CH = 448

COL_Q, COL_Z, COL_XL, COL_GL, COL_GATES = 0, 512, 1024, 1536, 2048
COL_XS, COL_DT, COL_BC, COL_QI, COL_K, COL_V, COL_KIWI = 5120, 5632, 6144, 6400, 6656, 6784, 6912
PROJ_W = 7168

LOG2_E = 1.4426950408889634
SHIFT_MARGIN = 1.01
ROWSUM_FLOOR = 2.0 ** -60
NEG_BIG = -1e30
KEY_NEG_INF = -2139095041
INT_MAX = 2147483647
INT_MIN = -2147483648


def _cparams(sem):
    return pltpu.CompilerParams(dimension_semantics=sem, vmem_limit_bytes=VMEM_LIMIT)


def _rms(x, g):
    ms = jnp.mean(x * x, axis=-1, keepdims=True)
    return x * lax.rsqrt(ms + NORM_EPS) * g


def _softplus(x):
    return jnp.maximum(x, 0.0) + jnp.log1p(jnp.exp(-jnp.abs(x)))


def _silu(x):
    return x * jax.nn.sigmoid(x)


def _norm_matmul_kernel(x_ref, g_ref, w_ref, o_ref, xn_ref):
    @pl.when(pl.program_id(1) == 0)
    def _():
        xn_ref[...] = _rms(x_ref[...], g_ref[...]).astype(BF16)

    o_ref[...] = jnp.dot(xn_ref[...], w_ref[...], preferred_element_type=F32)


def _norm_matmul(x, g, w, *, tm, tn):
    n, d = x.shape
    c = w.shape[1]
    return pl.pallas_call(
        _norm_matmul_kernel,
        grid=(n // tm, c // tn),
        in_specs=[pl.BlockSpec((tm, d), lambda i, j: (i, 0)),
                  pl.BlockSpec((1, d), lambda i, j: (0, 0)),
                  pl.BlockSpec((d, tn), lambda i, j: (0, j))],
        out_specs=pl.BlockSpec((tm, tn), lambda i, j: (i, j)),
        out_shape=jax.ShapeDtypeStruct((n, c), F32),
        scratch_shapes=[pltpu.VMEM((tm, d), BF16)],
        compiler_params=_cparams(("parallel", "arbitrary")),
        name="norm_in_proj",
    )(x, g, w)


def _rope_apply(x, cos, sin_signed, first_half):
    w = x.shape[1]
    reps = w // LANES
    if reps > 1:
        cos = jnp.concatenate([cos] * reps, axis=1)
        sin_signed = jnp.concatenate([sin_signed] * reps, axis=1)
        first_half = jnp.concatenate([first_half] * reps, axis=1)
    up = pltpu.roll(x, w - 32, axis=1)
    dn = pltpu.roll(x, 32, axis=1)
    return x * cos + jnp.where(first_half, up, dn) * sin_signed


def _rope_kernel(q_ref, qi_ref, k_ref, v_ref, kiwi_ref, cos_ref, sin_ref,
                 qo_ref, qio_ref, ko_ref, vo_ref, kio_ref, kbo_ref, vbo_ref, kibo_ref):
    cos = cos_ref[...]
    sin = sin_ref[...]
    lane = lax.broadcasted_iota(I32, cos.shape, 1)
    first_half = (lane % 64) < 32
    qo_ref[...] = _rope_apply(q_ref[...], cos, sin, first_half).astype(BF16)
    qio_ref[...] = _rope_apply(qi_ref[...], cos, sin, first_half).astype(BF16)
    k = _rope_apply(k_ref[...], cos, sin, first_half)
    ko_ref[...] = k
    kbo_ref[...] = k.astype(BF16)
    v = v_ref[...]
    vo_ref[...] = v
    vbo_ref[...] = v.astype(BF16)
    ki = _rope_apply(kiwi_ref[...], cos, sin, first_half)[:, :IDX_DIM]
    kio_ref[...] = ki
    kibo_ref[...] = ki.astype(BF16)


def _rope_call(proj, cos, sin, *, row0, nb, length, tl):
    nt = length // tl
    rb0 = row0 // tl
    n = nb * length

    def rows(b, j):
        return rb0 + b * nt + j

    def pspec(width, col):
        return pl.BlockSpec((tl, width), lambda b, j: (rows(b, j), col // width))

    def ospec(width):
        return pl.BlockSpec((tl, width), lambda b, j: (b * nt + j, 0))

    tspec = pl.BlockSpec((tl, LANES), lambda b, j: (j, 0))
    return pl.pallas_call(
        _rope_kernel,
        grid=(nb, nt),
        in_specs=[pspec(512, COL_Q), pspec(256, COL_QI), pspec(128, COL_K), pspec(128, COL_V),
                  pspec(128, COL_KIWI), tspec, tspec],
        out_specs=[ospec(512), ospec(256), ospec(128), ospec(128), ospec(IDX_DIM),
                   ospec(128), ospec(128), ospec(IDX_DIM)],
        out_shape=[jax.ShapeDtypeStruct((n, 512), BF16), jax.ShapeDtypeStruct((n, 256), BF16),
                   jax.ShapeDtypeStruct((n, 128), F32), jax.ShapeDtypeStruct((n, 128), F32),
                   jax.ShapeDtypeStruct((n, IDX_DIM), F32),
                   jax.ShapeDtypeStruct((n, 128), BF16), jax.ShapeDtypeStruct((n, 128), BF16),
                   jax.ShapeDtypeStruct((n, IDX_DIM), BF16)],
        compiler_params=_cparams(("parallel", "parallel")),
        name="rope",
    )(proj, proj, proj, proj, proj, cos, sin)


def _attn_kernel(q_ref, qi_ref, kiwi_ref, k_ref, v_ref, ki_ref, o_ref, key_ref, bias_ref,
                 plane_ref, act_ref, m_ref, acc_ref, knb_ref, *, tq, kb, s_valid, q_off, topk):
    i = pl.program_id(1)
    t0 = i * tq
    q_last = q_off + t0 + tq - 1
    n_adm = jnp.minimum((q_last // CHUNK + 1) * CHUNK, s_valid)
    nkb = (n_adm + kb - 1) // kb
    nsub = kb // LANES

    wi = kiwi_ref[:, IDX_DIM:IDX_DIM + IDX_HEADS]
    q_chunk = (q_off + t0 + lax.broadcasted_iota(I32, (tq, 1), 0)) // CHUNK
    nt_dims = (((1,), (1,)), ((), ()))

    @pl.when(i == 0)
    def _():
        if plane_ref.shape[0] > 32:
            plane_ref[32:] = jnp.zeros((plane_ref.shape[0] - 32,) + plane_ref.shape[1:], I32)
        for j in range(knb_ref.shape[0]):
            kf = k_ref[j * kb:(j + 1) * kb, :].astype(F32)
            ksq = jnp.sum(kf * kf, axis=1, keepdims=True)
            knb_ref[j] = jnp.broadcast_to(jnp.max(ksq, axis=0, keepdims=True), knb_ref.shape[1:])

    def kmax_body(j, m):
        return jnp.maximum(m, knb_ref[j])

    kmax2 = lax.fori_loop(0, nkb, kmax_body, jnp.zeros(knb_ref.shape[1:], F32))[0:1, 0:1]

    wib = [jnp.broadcast_to(wi[:, h:h + 1], (tq, LANES)) for h in range(IDX_HEADS)]

    def score_block(j, masked):
        off = pl.multiple_of(j * kb, kb)
        ki_blk = ki_ref[pl.ds(off, kb), :]
        sc = None
        for h in range(IDX_HEADS):
            qh = qi_ref[:, h * IDX_DIM:(h + 1) * IDX_DIM]
            s = lax.dot_general(qh, ki_blk, nt_dims, preferred_element_type=F32)
            wb = wib[h] if nsub == 1 else jnp.concatenate([wib[h]] * nsub, axis=1)
            term = jnp.maximum(s, 0.0) * wb
            sc = term if sc is None else sc + term
        sc = jnp.where(sc == 0.0, 0.0, sc)
        if masked:
            kpos = off + lax.broadcasted_iota(I32, (1, kb), 1)
            adm = ((kpos // CHUNK) <= q_chunk) & (kpos < s_valid)
            sc = jnp.where(adm, sc, -jnp.inf)
        bits = pltpu.bitcast(sc, I32)
        key_ref[j] = bits ^ ((bits >> 31) & INT_MAX)

    nfull = jnp.minimum(((q_off + t0) // CHUNK + 1) * CHUNK, s_valid) // kb

    def score_full(j, carry):
        score_block(j, False)
        return carry

    def score_edge(j, carry):
        score_block(j, True)
        return carry

    lax.fori_loop(0, nfull, score_full, 0)
    lax.fori_loop(nfull, nkb, score_edge, 0)

    bpg = 32 // nsub
    ng = (nkb + bpg - 1) // bpg
    nslab = tq // 8

    def fill_body(j, carry):
        key_ref[j] = jnp.full((tq, kb), INT_MIN, I32)
        return carry

    lax.fori_loop(nkb, ng * bpg, fill_body, 0)

    def transpose_body(idx, carry):
        g = idx // nslab
        r0 = pl.multiple_of((idx % nslab) * 8, 8)
        xs = [key_ref[g * bpg + t // nsub, pl.ds(r0, 8), (t % nsub) * LANES:(t % nsub + 1) * LANES]
              ^ INT_MIN for t in range(32)]
        j, m = 16, 0x0000FFFF
        while j:
            k = 0
            while k < 32:
                t = (xs[k] ^ lax.shift_right_logical(xs[k + j], j)) & m
                xs[k] = xs[k] ^ t
                xs[k + j] = xs[k + j] ^ (t << j)
                k = (k + j + 1) & ~j
            j >>= 1
            m = (m ^ (m << j)) & 0xFFFFFFFF
        for b in range(32):
            plane_ref[g * 32 + b, pl.ds(r0, 8), :] = xs[b]
        return carry

    lax.fori_loop(0, ng * nslab, transpose_body, 0)

    ng_max = act_ref.shape[0]
    for g in range(ng_max):
        act_ref[g] = jnp.broadcast_to(jnp.where(g < ng, -1, 0), (tq, LANES)).astype(I32)

    nhalf = 2 if tq % 16 == 0 and tq >= 128 else 1
    hrows = tq // nhalf

    def bit_body(step, carry):
        out = []
        for hf in range(nhalf):
            rem, thr_u = carry[2 * hf], carry[2 * hf + 1]
            rs = slice(hf * hrows, (hf + 1) * hrows)
            n1l = jnp.zeros((hrows, LANES), I32)
            for g in range(ng_max):
                n1l = n1l + lax.population_count(act_ref[g, rs, :] & plane_ref[g * 32 + step, rs, :])
            n1 = jnp.sum(n1l, axis=1, keepdims=True)
            take = n1 >= rem
            rem = jnp.where(take, rem, rem - n1)
            thr_u = jnp.where(take, thr_u | jnp.left_shift(jnp.int32(1), 31 - step), thr_u)
            for g in range(ng_max):
                a = act_ref[g, rs, :]
                w = a & plane_ref[g * 32 + step, rs, :]
                act_ref[g, rs, :] = jnp.where(take, w, a ^ w)
            out += [rem, thr_u]
        return tuple(out)

    init = (jnp.full((hrows, 1), topk, I32), jnp.zeros((hrows, 1), I32)) * nhalf
    fin = lax.fori_loop(0, 32, bit_body, init)
    rem = fin[0] if nhalf == 1 else jnp.concatenate(fin[0::2], axis=0)
    thr_u = fin[1] if nhalf == 1 else jnp.concatenate(fin[1::2], axis=0)
    thr = thr_u ^ INT_MIN

    eq_l = jnp.zeros((tq, LANES), I32)
    for g in range(ng_max):
        eq_l = eq_l + lax.population_count(act_ref[g])
    eq_cnt = jnp.sum(eq_l, axis=1, keepdims=True)
    has_ties = jnp.max(eq_cnt - rem) > 0

    @pl.when(jnp.logical_not(has_ties))
    def _():
        thr_eff = jnp.maximum(thr, KEY_NEG_INF + 1)

        def body(j, carry):
            bias_ref[j] = jnp.where(key_ref[j] >= thr_eff, 0.0, NEG_BIG)
            return carry

        lax.fori_loop(0, nkb, body, 0)

    @pl.when(has_ties)
    def _():
        need = rem.astype(F32)
        thr_ok = thr > KEY_NEG_INF
        r = lax.broadcasted_iota(I32, (kb, kb), 0)
        c = lax.broadcasted_iota(I32, (kb, kb), 1)
        upper = jnp.where(r < c, 1.0, 0.0).astype(BF16)

        def body(j, carry):
            key = key_ref[j]
            gt = key > thr
            eq = (key == thr) & thr_ok
            eqf = jnp.where(eq, 1.0, 0.0)
            rank = jnp.dot(eqf.astype(BF16), upper, preferred_element_type=F32) + carry
            sel = gt | (eq & (rank < need))
            bias_ref[j] = jnp.where(sel, 0.0, NEG_BIG)
            return carry + jnp.sum(eqf, axis=1, keepdims=True)

        lax.fori_loop(0, nkb, body, jnp.zeros((tq, 1), F32))

    hpg = ATT_HEADS // KV_HEADS

    def logits(j, h, kblk):
        qh = q_ref[:, h * HEAD_DIM:(h + 1) * HEAD_DIM]
        s = lax.dot_general(qh, kblk, nt_dims, preferred_element_type=F32)
        return s + bias_ref[j]

    def exp_and_values():
        acc_ref[...] = jnp.zeros(acc_ref.shape, F32)

        def pv_body(j, carry):
            off = pl.multiple_of(j * kb, kb)
            for g in range(KV_HEADS):
                kblk = k_ref[pl.ds(off, kb), g * HEAD_DIM:(g + 1) * HEAD_DIM]
                vblk = v_ref[pl.ds(off, kb), g * LANES:(g + 1) * LANES]
                for hh in range(hpg):
                    h = g * hpg + hh
                    s = logits(j, h, kblk)
                    mb = m_ref[h]
                    ps = [jnp.exp2(s[:, c * LANES:(c + 1) * LANES] - mb) for c in range(nsub)]
                    p = jnp.concatenate(ps, axis=1) if nsub > 1 else ps[0]
                    acc_ref[h] += jnp.dot(p.astype(BF16), vblk, preferred_element_type=F32)
            return carry

        lax.fori_loop(0, nkb, pv_body, 0)

    qf = q_ref[...].astype(F32)
    for h in range(ATT_HEADS):
        qh = qf[:, h * HEAD_DIM:(h + 1) * HEAD_DIM]
        bound = jnp.sqrt(jnp.sum(qh * qh, axis=1, keepdims=True) * kmax2) * SHIFT_MARGIN
        m_ref[h] = jnp.broadcast_to(bound, (tq, LANES))
    exp_and_values()
    lmin = jnp.min(acc_ref[0][:, HEAD_DIM:HEAD_DIM + 1])
    for h in range(1, ATT_HEADS):
        lmin = jnp.minimum(lmin, jnp.min(acc_ref[h][:, HEAD_DIM:HEAD_DIM + 1]))

    @pl.when(jnp.logical_not(lmin >= ROWSUM_FLOOR))
    def _():
        m_ref[...] = jnp.full(m_ref.shape, NEG_BIG, F32)

        def max_body(j, carry):
            off = pl.multiple_of(j * kb, kb)
            for g in range(KV_HEADS):
                kblk = k_ref[pl.ds(off, kb), g * HEAD_DIM:(g + 1) * HEAD_DIM]
                for hh in range(hpg):
                    h = g * hpg + hh
                    s = logits(j, h, kblk)
                    mt = s[:, 0:LANES]
                    for c in range(1, nsub):
                        mt = jnp.maximum(mt, s[:, c * LANES:(c + 1) * LANES])
                    m_ref[h] = jnp.maximum(m_ref[h], mt)
            return carry

        lax.fori_loop(0, nkb, max_body, 0)
        for h in range(ATT_HEADS):
            m_ref[h] = jnp.broadcast_to(jnp.max(m_ref[h], axis=1, keepdims=True), (tq, LANES))
        exp_and_values()

    for h in range(ATT_HEADS):
        acc = acc_ref[h]
        o_ref[:, h * HEAD_DIM:(h + 1) * HEAD_DIM] = (
            acc[:, :HEAD_DIM] / acc[:, HEAD_DIM:HEAD_DIM + 1]).astype(BF16)


def _attn_call(q, qi, proj, k, v, ki, *, row0, nb, length, s_pad, s_valid, q_off, tq, kb, topk):
    nq = length // tq
    rb0 = row0 // tq
    nkb_max = s_pad // kb
    bpg = 32 // (kb // LANES)
    ng_max = -(-nkb_max // bpg)
    kern = functools.partial(_attn_kernel, tq=tq, kb=kb, s_valid=s_valid, q_off=q_off, topk=topk)
    return pl.pallas_call(
        kern,
        grid=(nb, nq),
        in_specs=[pl.BlockSpec((tq, 512), lambda b, i: (b * nq + i, 0)),
                  pl.BlockSpec((tq, 256), lambda b, i: (b * nq + i, 0)),
                  pl.BlockSpec((tq, 128), lambda b, i: (rb0 + b * nq + i, COL_KIWI // 128)),
                  pl.BlockSpec((s_pad, 128), lambda b, i: (b, 0)),
                  pl.BlockSpec((s_pad, 2 * LANES), lambda b, i: (b, 0)),
                  pl.BlockSpec((s_pad, IDX_DIM), lambda b, i: (b, 0))],
        out_specs=pl.BlockSpec((tq, 512), lambda b, i: (b * nq + i, 0)),
        out_shape=jax.ShapeDtypeStruct((nb * length, 512), BF16),
        scratch_shapes=[pltpu.VMEM((ng_max * bpg, tq, kb), I32), pltpu.VMEM((nkb_max, tq, kb), F32),
                        pltpu.VMEM((ng_max * 32, tq, LANES), I32), pltpu.VMEM((ng_max, tq, LANES), I32),
                        pltpu.VMEM((ATT_HEADS, tq, LANES), F32), pltpu.VMEM((ATT_HEADS, tq, LANES), F32),
                        pltpu.VMEM((nkb_max, 8, LANES), F32)],
        compiler_params=_cparams(("parallel", "arbitrary")),
        name="dsa_attention",
    )(q, qi, proj, k, v, ki)


def _ssm_kernel(z_ref, xs_ref, dt_ref, bc_ref, cx0_ref, cbc0_ref, s0_ref,
                cwx_ref, cwbc_ref, cbx_ref, cbbc_ref, dtb_ref, alog_ref, dsk_ref, nw_ref,
                y_ref, sT_ref, cxT_ref, cbcT_ref,
                xpx_ref, xpbc_ref, xc_ref, bcc_ref, dtc_ref, ypre_ref, st_ref,
                *, t_in, t_pad):
    j = pl.program_id(1)
    nj = pl.num_programs(1)
    q = CHUNK

    @pl.when(j == 0)
    def _():
        xpx_ref[0:8, :] = cx0_ref[...]
        xpbc_ref[0:8, :] = cbc0_ref[...]
        st_ref[...] = s0_ref[...]

    xpx_ref[8:8 + t_in, :] = xs_ref[...]
    xpbc_ref[8:8 + t_in, :] = bc_ref[...]

    def conv(xp_ref, w_ref, b_ref):
        y = b_ref[...]
        for tap in range(4):
            y = y + xp_ref[5 + tap:5 + tap + t_in, :] * w_ref[tap:tap + 1, :]
        return _silu(y)

    if t_pad > t_in:
        xc_ref[...] = jnp.zeros_like(xc_ref)
        bcc_ref[...] = jnp.zeros_like(bcc_ref)
        dtc_ref[...] = jnp.zeros_like(dtc_ref)
    xc_ref[0:t_in, :] = conv(xpx_ref, cwx_ref, cbx_ref)
    bcc_ref[0:t_in, :] = conv(xpbc_ref, cwbc_ref, cbbc_ref)
    dtc_ref[0:t_in, :] = _softplus(dt_ref[...] + dtb_ref[...])

    last_x = xpx_ref[t_in:t_in + 8, :]
    last_bc = xpbc_ref[t_in:t_in + 8, :]
    xpx_ref[0:8, :] = last_x
    xpbc_ref[0:8, :] = last_bc

    a_neg = -jnp.exp(alog_ref[...])
    li = lax.broadcasted_iota(I32, (q, q), 0)
    si = lax.broadcasted_iota(I32, (q, q), 1)
    tri = jnp.where(si <= li, 1.0, 0.0).astype(BF16)
    ones = jnp.ones((q, q), BF16)
    lane = lax.broadcasted_iota(I32, (q, MIX), 1)
    row = lax.broadcasted_iota(I32, (q, MIX), 0)
    s_of_lane = lane % q
    mask_t_le_s = jnp.where(row <= s_of_lane, 1.0, 0.0)
    causal = s_of_lane <= row
    rg = lax.broadcasted_iota(I32, (SSM_HEADS * q, 2 * SSM_STATE), 0) // (q * SSM_HEADS // SSM_GROUPS)
    cg = lax.broadcasted_iota(I32, (SSM_HEADS * q, 2 * SSM_STATE), 1) // SSM_STATE
    gmask = rg == cg
    rh = lax.broadcasted_iota(I32, (SSM_HEADS * q, MIX), 0) // q
    ch = lax.broadcasted_iota(I32, (SSM_HEADS * q, MIX), 1) // SSM_HEAD_DIM
    hmask = rh == ch
    r2 = lax.broadcasted_iota(I32, (2 * SSM_STATE, MIX), 0) // SSM_STATE
    c2 = lax.broadcasted_iota(I32, (2 * SSM_STATE, MIX), 1) // (MIX // SSM_GROUPS)
    g2mask = r2 == c2
    nt_dims = (((1,), (1,)), ((), ()))
    tn_dims = (((0,), (0,)), ((), ()))

    def sum_rows(sel01, a):
        a_hi = a.astype(BF16)
        r1 = a - a_hi.astype(F32)
        a_mid = r1.astype(BF16)
        a_lo = (r1 - a_mid.astype(F32)).astype(BF16)
        return (jnp.dot(sel01, a_hi, preferred_element_type=F32)
                + jnp.dot(sel01, a_mid, preferred_element_type=F32)
                + jnp.dot(sel01, a_lo, preferred_element_type=F32))

    def chunk_body(c, carry):
        r0 = pl.multiple_of(c * q, q)
        xs = xc_ref[pl.ds(r0, q), :]
        dt = dtc_ref[pl.ds(r0, q), :]
        bmat = bcc_ref[pl.ds(r0, q), 0:2 * SSM_STATE]
        cmat = bcc_ref[pl.ds(r0, q), 2 * SSM_STATE:4 * SSM_STATE]
        a = dt * a_neg
        xdt = xs * dt
        acum = sum_rows(tri, a)
        rowt = sum_rows(ones, a * mask_t_le_s)
        decay_in = jnp.where(causal, jnp.exp(acum - rowt), 0.0)
        bexp = jnp.where(gmask, jnp.concatenate([bmat] * SSM_HEADS, axis=0), 0.0)
        cb = lax.dot_general(cmat.astype(BF16), bexp.astype(BF16), nt_dims,
                             preferred_element_type=F32)
        m = (cb * decay_in).astype(BF16)
        bdx = jnp.where(hmask, jnp.concatenate([xdt] * SSM_HEADS, axis=0), 0.0).astype(BF16)
        y_diag = jnp.dot(m, bdx, preferred_element_type=F32)
        st = st_ref[...]
        y_off = jnp.exp(acum) * jnp.dot(cmat.astype(BF16), st.astype(BF16),
                                        preferred_element_type=F32)
        a_end = acum[q - 1:q, :]
        xd = (xdt * jnp.exp(a_end - acum)).astype(BF16)
        upd = lax.dot_general(bmat.astype(BF16), xd, tn_dims, preferred_element_type=F32)
        st_ref[...] = jnp.exp(a_end) * st + jnp.where(g2mask, upd, 0.0)
        ypre_ref[pl.ds(r0, q), :] = y_diag + y_off
        return carry

    lax.fori_loop(0, t_pad // q, chunk_body, 0)

    xs = xc_ref[0:t_in, :]
    y = ypre_ref[0:t_in, :] + dsk_ref[...] * xs
    y = y * _silu(z_ref[...])
    half = MIX // SSM_GROUPS
    parts = []
    for g in range(SSM_GROUPS):
        yg = y[:, g * half:(g + 1) * half]
        parts.append(yg * lax.rsqrt(jnp.mean(yg * yg, axis=-1, keepdims=True) + NORM_EPS))
    y = jnp.concatenate(parts, axis=1) * nw_ref[...]
    y_ref[...] = y.astype(BF16)

    @pl.when(j == nj - 1)
    def _():
        sT_ref[...] = st_ref[...]
        cxT_ref[...] = last_x
        cbcT_ref[...] = last_bc


def _ssm_call(proj, cx0, cbc0, s0, wts, *, row0, nb, length, tl):
    nt = length // tl
    rb0 = row0 // tl
    t_pad = -(-tl // CHUNK) * CHUNK
    kern = functools.partial(_ssm_kernel, t_in=tl, t_pad=t_pad)

    def pspec(width, col):
        return pl.BlockSpec((tl, width), lambda b, j: (rb0 + b * nt + j, col // width))

    def bspec(r, c):
        return pl.BlockSpec((None, r, c), lambda b, j: (b, 0, 0))

    def wspec(r, c):
        return pl.BlockSpec((r, c), lambda b, j: (0, 0))

    return pl.pallas_call(
        kern,
        grid=(nb, nt),
        in_specs=[pspec(512, COL_Z), pspec(512, COL_XS), pspec(512, COL_DT), pspec(256, COL_BC),
                  bspec(8, MIX), bspec(8, SSM_BC), bspec(2 * SSM_STATE, MIX),
                  wspec(4, MIX), wspec(4, SSM_BC), wspec(1, MIX), wspec(1, SSM_BC),
                  wspec(1, MIX), wspec(1, MIX), wspec(1, MIX), wspec(1, MIX)],
        out_specs=[pl.BlockSpec((tl, MIX), lambda b, j: (b * nt + j, 0)),
                   bspec(2 * SSM_STATE, MIX), bspec(8, MIX), bspec(8, SSM_BC)],
        out_shape=[jax.ShapeDtypeStruct((nb * length, MIX), BF16),
                   jax.ShapeDtypeStruct((nb, 2 * SSM_STATE, MIX), F32),
                   jax.ShapeDtypeStruct((nb, 8, MIX), F32),
                   jax.ShapeDtypeStruct((nb, 8, SSM_BC), F32)],
        scratch_shapes=[pltpu.VMEM((tl + 8, MIX), F32), pltpu.VMEM((tl + 8, SSM_BC), F32),
                        pltpu.VMEM((t_pad, MIX), F32), pltpu.VMEM((t_pad, SSM_BC), F32),
                        pltpu.VMEM((t_pad, MIX), F32), pltpu.VMEM((t_pad, MIX), F32),
                        pltpu.VMEM((2 * SSM_STATE, MIX), F32)],
        compiler_params=_cparams(("parallel", "arbitrary")),
        name="ssd_branch",
    )(proj, proj, proj, proj, cx0, cbc0, s0, *wts)


def _lru_kernel(xl_ref, gl_ref, c0_ref, h0_ref, cw_ref, cb_ref, wa_ref, ba_ref, wx_ref, bx_ref,
                lam_ref, y_ref, hT_ref, cT_ref, xp_ref, a_ref, u_ref, hs_ref, h_ref, *, tl):
    j = pl.program_id(1)
    nj = pl.num_programs(1)

    @pl.when(j == 0)
    def _():
        xp_ref[0:8, :] = c0_ref[...]
        h_ref[...] = h0_ref[...]

    xp_ref[8:8 + tl, :] = xl_ref[...]
    xc = cb_ref[...]
    for tap in range(4):
        xc = xc + xp_ref[5 + tap:5 + tap + tl, :] * cw_ref[tap:tap + 1, :]
    last = xp_ref[tl:tl + 8, :]
    xp_ref[0:8, :] = last

    xcb = xc.astype(BF16)
    r = jax.nn.sigmoid(jnp.dot(xcb, wa_ref[...], preferred_element_type=F32) + ba_ref[...])
    i = jax.nn.sigmoid(jnp.dot(xcb, wx_ref[...], preferred_element_type=F32) + bx_ref[...])
    log_a = (-LRU_C * _softplus(-lam_ref[...])) * r
    a = jnp.exp(log_a)
    mult = jnp.sqrt(-jnp.tanh(log_a) * (a * a + 1.0))
    a_ref[...] = a
    u_ref[...] = mult * (i * xc)

    def step(t, h):
        h = a_ref[pl.ds(t, 1), :] * h + u_ref[pl.ds(t, 1), :]
        hs_ref[pl.ds(t, 1), :] = h
        return h

    h = lax.fori_loop(0, tl, step, h_ref[...], unroll=8)
    h_ref[...] = h
    y_ref[...] = (hs_ref[...] * jax.nn.gelu(gl_ref[...], approximate=True)).astype(BF16)

    @pl.when(j == nj - 1)
    def _():
        hT_ref[...] = h
        cT_ref[...] = last


def _lru_call(proj, c0, h0, wts, *, row0, nb, length, tl):
    nt = length // tl
    rb0 = row0 // tl
    kern = functools.partial(_lru_kernel, tl=tl)

    def pspec(col):
        return pl.BlockSpec((tl, MIX), lambda b, j: (rb0 + b * nt + j, col // MIX))

    def bspec(r):
        return pl.BlockSpec((None, r, MIX), lambda b, j: (b, 0, 0))

    def wspec(r):
        return pl.BlockSpec((r, MIX), lambda b, j: (0, 0))

    return pl.pallas_call(
        kern,
        grid=(nb, nt),
        in_specs=[pspec(COL_XL), pspec(COL_GL), bspec(8), bspec(1),
                  wspec(4), wspec(1), wspec(MIX), wspec(1), wspec(MIX), wspec(1), wspec(1)],
        out_specs=[pl.BlockSpec((tl, MIX), lambda b, j: (b * nt + j, 0)), bspec(1), bspec(8)],
        out_shape=[jax.ShapeDtypeStruct((nb * length, MIX), BF16),
                   jax.ShapeDtypeStruct((nb, 1, MIX), F32),
                   jax.ShapeDtypeStruct((nb, 8, MIX), F32)],
        scratch_shapes=[pltpu.VMEM((tl + 8, MIX), F32), pltpu.VMEM((tl, MIX), F32),
                        pltpu.VMEM((tl, MIX), F32), pltpu.VMEM((tl, MIX), F32),
                        pltpu.VMEM((1, MIX), F32)],
        compiler_params=_cparams(("parallel", "arbitrary")),
        name="rglru_branch",
    )(proj, proj, c0, h0, *wts)


def _merge_kernel(x_ref, attp_ref, ssmp_ref, lrup_ref, atts_ref, ssms_ref, lrus_ref,
                  g0_ref, g1_ref, g2_ref, wa_ref, ws_ref, wl_ref, wo_ref, o_ref, *, prompt_tiles):
    is_prompt = pl.program_id(0) < prompt_tiles
    att = jnp.where(is_prompt, attp_ref[...], atts_ref[...])
    ssm = jnp.where(is_prompt, ssmp_ref[...], ssms_ref[...])
    lru = jnp.where(is_prompt, lrup_ref[...], lrus_ref[...])
    merged = (jax.nn.sigmoid(g0_ref[...]) * jnp.dot(att, wa_ref[...], preferred_element_type=F32)
              + jax.nn.sigmoid(g1_ref[...]) * jnp.dot(ssm, ws_ref[...], preferred_element_type=F32)
              + jax.nn.sigmoid(g2_ref[...]) * jnp.dot(lru, wl_ref[...], preferred_element_type=F32))
    o_ref[...] = x_ref[...] + jnp.dot(merged.astype(BF16), wo_ref[...], preferred_element_type=F32)


def _merge_call(x, branches_p, branches_s, proj, wa, ws, wl, wo, *, tm):
    n, d = x.shape
    n_p = branches_p[0].shape[0]
    n_s = branches_s[0].shape[0]
    assert n_p % tm == 0 and n_s % tm == 0 and n_p + n_s == n
    pt = n_p // tm
    st = n_s // tm

    def rspec(width):
        return pl.BlockSpec((tm, width), lambda i: (i, 0))

    pspec = pl.BlockSpec((tm, MIX), lambda i: (jnp.minimum(i, pt - 1), 0))
    sspec = pl.BlockSpec((tm, MIX), lambda i: (jnp.clip(i - pt, 0, st - 1), 0))

    def gspec(k):
        return pl.BlockSpec((tm, d), lambda i: (i, COL_GATES // d + k))

    def wspec(r):
        return pl.BlockSpec((r, d), lambda i: (0, 0))

    return pl.pallas_call(
        functools.partial(_merge_kernel, prompt_tiles=pt),
        grid=(n // tm,),
        in_specs=[rspec(d), pspec, pspec, pspec, sspec, sspec, sspec, gspec(0), gspec(1), gspec(2),
                  wspec(MIX), wspec(MIX), wspec(MIX), wspec(d)],
        out_specs=rspec(d),
        out_shape=jax.ShapeDtypeStruct((n, d), F32),
        compiler_params=_cparams(("parallel",)),
        name="branch_merge",
    )(x, *branches_p, *branches_s, proj, proj, proj, wa, ws, wl, wo)


def _ffn_kernel(x_ref, g_ref, w1_ref, w3_ref, w2_ref, o_ref, xn_ref, acc_ref):
    f = pl.program_id(1)

    @pl.when(f == 0)
    def _():
        xn_ref[...] = _rms(x_ref[...], g_ref[...]).astype(BF16)
        acc_ref[...] = jnp.zeros_like(acc_ref)

    xn = xn_ref[...]
    h1 = jnp.dot(xn, w1_ref[...], preferred_element_type=F32)
    h3 = jnp.dot(xn, w3_ref[...], preferred_element_type=F32)
    h = (_silu(h1) * h3).astype(BF16)
    acc_ref[...] += jnp.dot(h, w2_ref[...], preferred_element_type=F32)

    @pl.when(f == pl.num_programs(1) - 1)
    def _():
        o_ref[...] = x_ref[...] + acc_ref[...]


def _ffn_call(x, g, w1, w3, w2, *, tm, tf):
    n, d = x.shape
    dff = w1.shape[1]
    return pl.pallas_call(
        _ffn_kernel,
        grid=(n // tm, dff // tf),
        in_specs=[pl.BlockSpec((tm, d), lambda i, f: (i, 0)),
                  pl.BlockSpec((1, d), lambda i, f: (0, 0)),
                  pl.BlockSpec((d, tf), lambda i, f: (0, f)),
                  pl.BlockSpec((d, tf), lambda i, f: (0, f)),
                  pl.BlockSpec((tf, d), lambda i, f: (f, 0))],
        out_specs=pl.BlockSpec((tm, d), lambda i, f: (i, 0)),
        out_shape=jax.ShapeDtypeStruct((n, d), F32),
        scratch_shapes=[pltpu.VMEM((tm, d), BF16), pltpu.VMEM((tm, d), F32)],
        compiler_params=_cparams(("parallel", "arbitrary")),
        name="swiglu_ffn",
    )(x, g, w1, w3, w2)


def _router_kernel(x_ref, g_ref, wr_ref, gate_ref, xn_ref, slot_ref, cnt_ref, run_ref, *, sub):
    i = pl.program_id(0)
    xn = _rms(x_ref[...], g_ref[...])
    xn_ref[...] = xn.astype(BF16)
    logits = jnp.dot(xn, wr_ref[...], precision=lax.Precision.HIGHEST, preferred_element_type=F32)
    lane = lax.broadcasted_iota(I32, logits.shape, 1)
    logits = jnp.where(lane < N_EXPERTS, logits, -jnp.inf)
    m1 = jnp.max(logits, axis=1, keepdims=True)
    i1 = jnp.min(jnp.where(logits == m1, lane, LANES), axis=1, keepdims=True)
    rest = jnp.where(lane == i1, -jnp.inf, logits)
    m2 = jnp.max(rest, axis=1, keepdims=True)
    i2 = jnp.min(jnp.where(rest == m2, lane, LANES), axis=1, keepdims=True)
    e2 = jnp.exp(m2 - m1)
    den = 1.0 + e2
    gate = jnp.where(lane == i1, 1.0 / den, 0.0) + jnp.where(lane == i2, e2 / den, 0.0)
    gate_ref[...] = gate

    @pl.when(i % sub == 0)
    def _():
        run_ref[...] = jnp.zeros_like(run_ref)

    tm = gate.shape[0]
    routed = jnp.where(gate != 0.0, 1.0, 0.0)
    r = lax.broadcasted_iota(I32, (tm, tm), 0)
    c = lax.broadcasted_iota(I32, (tm, tm), 1)
    earlier = jnp.where(c < r, 1.0, 0.0).astype(BF16)
    rank = jnp.dot(earlier, routed.astype(BF16), preferred_element_type=F32) + run_ref[0:1, :]
    slot_ref[...] = jnp.where(routed > 0.0, rank, -1.0)
    total = run_ref[0:1, :] + jnp.sum(routed, axis=0, keepdims=True)
    run_ref[...] = jnp.broadcast_to(total, run_ref.shape)
    cnt_ref[...] = jnp.broadcast_to(total, cnt_ref.shape)


def _router_call(x, g, wr, *, tm, moe_tm):
    n, d = x.shape
    sub = moe_tm // tm
    return pl.pallas_call(
        functools.partial(_router_kernel, sub=sub),
        grid=(n // tm,),
        in_specs=[pl.BlockSpec((tm, d), lambda i: (i, 0)),
                  pl.BlockSpec((1, d), lambda i: (0, 0)),
                  pl.BlockSpec((d, LANES), lambda i: (0, 0))],
        out_specs=[pl.BlockSpec((tm, LANES), lambda i: (i, 0)), pl.BlockSpec((tm, d), lambda i: (i, 0)),
                   pl.BlockSpec((tm, LANES), lambda i: (i, 0)),
                   pl.BlockSpec((None, 8, LANES), lambda i: (i // sub, 0, 0))],
        out_shape=[jax.ShapeDtypeStruct((n, LANES), F32), jax.ShapeDtypeStruct((n, d), BF16),
                   jax.ShapeDtypeStruct((n, LANES), F32),
                   jax.ShapeDtypeStruct((n // moe_tm, 8, LANES), F32)],
        scratch_shapes=[pltpu.VMEM((8, LANES), F32)],
        compiler_params=_cparams(("arbitrary",)),
        name="moe_router",
    )(x, g, wr)


def _moe_expert_kernel(nch_ref, slot_ref, xn_ref, w1_ref, w3_ref, w2_ref, yc_ref, xg_ref, yacc_ref,
                       *, ch):
    e = pl.program_id(0)
    t = pl.program_id(1)
    f = pl.program_id(2)
    nch = nch_ref[t * pl.num_programs(0) + e]

    @pl.when(f == 0)
    def _():
        srow = slot_ref[pl.ds(e, 1), :]

        def gather(c, carry):
            rows = c * ch + lax.broadcasted_iota(I32, (ch, 1), 0)
            onehot = jnp.where(srow == rows, 1.0, 0.0).astype(BF16)
            xg_ref[c] = jnp.dot(onehot, xn_ref[...], preferred_element_type=F32).astype(BF16)
            yacc_ref[c] = jnp.zeros((ch, yacc_ref.shape[2]), F32)
            return carry

        lax.fori_loop(0, nch, gather, 0)

    def ffn(c, carry):
        xg = xg_ref[c]
        h1 = jnp.dot(xg, w1_ref[...], preferred_element_type=F32)
        h3 = jnp.dot(xg, w3_ref[...], preferred_element_type=F32)
        h = (_silu(h1) * h3).astype(BF16)
        yacc_ref[c] += jnp.dot(h, w2_ref[...], preferred_element_type=F32)
        return carry

    lax.fori_loop(0, nch, ffn, 0)

    @pl.when(f == pl.num_programs(2) - 1)
    def _():
        def emit(c, carry):
            yc_ref[pl.ds(pl.multiple_of(c * ch, ch), ch), :] = yacc_ref[c].astype(BF16)
            return carry

        def clear(c, carry):
            yc_ref[pl.ds(pl.multiple_of(c * ch, ch), ch), :] = jnp.zeros((ch, yc_ref.shape[1]), BF16)
            return carry

        lax.fori_loop(0, nch, emit, 0)
        lax.fori_loop(nch, yc_ref.shape[0] // ch, clear, 0)


def _moe_expert_call(nch, slot_exp, xn, w1, w3, w2, *, tm, tf, ch):
    n, d = xn.shape
    ne, _, dff = w1.shape
    nt = n // tm
    nch_max = -(-tm // ch)
    kern = functools.partial(_moe_expert_kernel, ch=ch)
    return pl.pallas_call(
        kern,
        grid_spec=pltpu.PrefetchScalarGridSpec(
            num_scalar_prefetch=1,
            grid=(ne, nt, dff // tf),
            in_specs=[pl.BlockSpec((None, ne, tm), lambda e, t, f, nch: (t, 0, 0)),
                      pl.BlockSpec((tm, d), lambda e, t, f, nch: (t, 0)),
                      pl.BlockSpec((None, d, tf), lambda e, t, f, nch: (e, 0, f)),
                      pl.BlockSpec((None, d, tf), lambda e, t, f, nch: (e, 0, f)),
                      pl.BlockSpec((None, tf, d), lambda e, t, f, nch: (e, f, 0))],
            out_specs=pl.BlockSpec((None, None, nch_max * ch, d), lambda e, t, f, nch: (e, t, 0, 0)),
            scratch_shapes=[pltpu.VMEM((nch_max, ch, d), BF16), pltpu.VMEM((nch_max, ch, d), F32)]),
        out_shape=jax.ShapeDtypeStruct((ne, nt, nch_max * ch, d), BF16),
        compiler_params=_cparams(("parallel", "parallel", "arbitrary")),
        name="moe_experts",
    )(nch, slot_exp, xn, w1, w3, w2)


def _moe_combine_kernel(nch_ref, x_ref, slot_ref, gate_ref, yc_ref, g_ref, o_ref, *, ch, final_norm):
    t = pl.program_id(0)
    e = pl.program_id(1)
    nch = nch_ref[t * pl.num_programs(1) + e]

    @pl.when(e == 0)
    def _():
        o_ref[...] = x_ref[...]

    lane = lax.broadcasted_iota(I32, slot_ref.shape, 1)
    scol = jnp.sum(jnp.where(lane == e, slot_ref[...], 0.0), axis=1, keepdims=True)
    gcol = jnp.sum(jnp.where(lane == e, gate_ref[...], 0.0), axis=1, keepdims=True)

    def scatter(c, carry):
        cols = (c * ch + lax.broadcasted_iota(I32, (1, ch), 1)).astype(F32)
        onehot = jnp.where(scol == cols, 1.0, 0.0).astype(BF16)
        yc = yc_ref[pl.ds(pl.multiple_of(c * ch, ch), ch), :]
        o_ref[...] += gcol * jnp.dot(onehot, yc, preferred_element_type=F32)
        return carry

    lax.fori_loop(0, nch, scatter, 0)

    if final_norm:
        @pl.when(e == pl.num_programs(1) - 1)
        def _():
            o_ref[...] = _rms(o_ref[...], g_ref[...])


def _moe_combine_call(nch, x, slot_tok, gate, yc, g_final, *, tm, ch, final_norm):
    n, d = x.shape
    ne, nt, rows, _ = yc.shape
    kern = functools.partial(_moe_combine_kernel, ch=ch, final_norm=final_norm)
    return pl.pallas_call(
        kern,
        grid_spec=pltpu.PrefetchScalarGridSpec(
            num_scalar_prefetch=1,
            grid=(nt, ne),
            in_specs=[pl.BlockSpec((tm, d), lambda t, e, nch: (t, 0)),
                      pl.BlockSpec((tm, LANES), lambda t, e, nch: (t, 0)),
                      pl.BlockSpec((tm, LANES), lambda t, e, nch: (t, 0)),
                      pl.BlockSpec((None, None, rows, d), lambda t, e, nch: (e, t, 0, 0)),
                      pl.BlockSpec((1, d), lambda t, e, nch: (0, 0))],
            out_specs=pl.BlockSpec((tm, d), lambda t, e, nch: (t, 0))),
        out_shape=jax.ShapeDtypeStruct((n, d), F32),
        compiler_params=_cparams(("parallel", "arbitrary")),
        name="moe_combine",
    )(nch, x, slot_tok, gate, yc, g_final)


def _moe_routing_tables(slot_tok, counts, *, tm, ch):
    n = slot_tok.shape[0]
    nt = n // tm
    cnt = counts[:, 0, :N_EXPERTS].astype(I32)
    nch = ((cnt + ch - 1) // ch).reshape(nt * N_EXPERTS)
    slot_exp = slot_tok[:, :N_EXPERTS].astype(I32).reshape(nt, tm, N_EXPERTS).transpose(0, 2, 1)
    return nch, slot_exp


def _norm_kernel(x_ref, g_ref, o_ref):
    o_ref[...] = _rms(x_ref[...], g_ref[...])


def _norm_call(x, g, *, tm):
    n, d = x.shape
    return pl.pallas_call(
        _norm_kernel,
        grid=(n // tm,),
        in_specs=[pl.BlockSpec((tm, d), lambda i: (i, 0)), pl.BlockSpec((1, d), lambda i: (0, 0))],
        out_specs=pl.BlockSpec((tm, d), lambda i: (i, 0)),
        out_shape=jax.ShapeDtypeStruct((n, d), F32),
        compiler_params=_cparams(("parallel",)),
        name="final_norm",
    )(x, g)


def _pack_w_in(w):
    d = w.shape[0]
    o = np.cumsum([0, 512, 128, 128, 256, 64, 4, 512, 768, 8, 512, 512, 3072])
    seg = lambda k: w[:, int(o[k]):int(o[k + 1])]
    q, k, v, qi, ki, wi, z, xbc, dt, xl, gl, gates = [seg(t) for t in range(12)]
    xs, bc = xbc[:, :MIX], xbc[:, MIX:]
    dt_exp = jnp.repeat(dt, SSM_HEAD_DIM, axis=1)
    pieces = ((COL_Q, q * (HEAD_DIM ** -0.5 * LOG2_E)), (COL_Z, z), (COL_XL, xl), (COL_GL, gl),
              (COL_GATES, gates), (COL_XS, xs), (COL_DT, dt_exp), (COL_BC, bc), (COL_QI, qi),
              (COL_K, k), (COL_V, v), (COL_KIWI, ki), (COL_KIWI + IDX_DIM, wi))
    packed = jnp.zeros((d, PROJ_W), BF16)
    for col, piece in pieces:
        packed = lax.dynamic_update_slice(packed, piece.astype(BF16), (0, col))
    return packed


def _block_diag(w):
    nblk, bw, _ = w.shape
    eye = jnp.eye(nblk, dtype=w.dtype)
    return jnp.einsum('kij,kl->kilj', w, eye).reshape(nblk * bw, nblk * bw)


def _rope_tables(pos):
    half = HEAD_DIM // 2
    inv = 1.0 / (ROPE_THETA ** (jnp.arange(half, dtype=F32) / half))
    ang = pos.astype(F32)[:, None] * inv[None, :]
    cos, sin = jnp.cos(ang), jnp.sin(ang)
    cos_t = jnp.concatenate([cos, cos, cos, cos], axis=1)
    sin_t = jnp.concatenate([-sin, sin, -sin, sin], axis=1)
    return cos_t, sin_t


def _pad_rows8(a):
    return jnp.pad(a, ((0, 0), (5, 0), (0, 0)))


def _state_to_s2(h):
    nb = h.shape[0]
    hg = h.reshape(nb, SSM_GROUPS, SSM_HEADS // SSM_GROUPS, SSM_HEAD_DIM, SSM_STATE)
    eye = jnp.eye(SSM_GROUPS, dtype=h.dtype)
    s2 = jnp.einsum('bgkpn,gf->bfngkp', hg, eye)
    return s2.reshape(nb, SSM_GROUPS * SSM_STATE, MIX)


def _s2_to_state(s2):
    nb = s2.shape[0]
    s6 = s2.reshape(nb, SSM_GROUPS, SSM_STATE, SSM_GROUPS, SSM_HEADS // SSM_GROUPS, SSM_HEAD_DIM)
    diag = jnp.stack([s6[:, g, :, g] for g in range(SSM_GROUPS)], axis=1)
    return diag.transpose(0, 1, 3, 4, 2).reshape(nb, SSM_HEADS, SSM_HEAD_DIM, SSM_STATE)


def _with_ones_column(v):
    n = v.shape[0]
    e = jnp.zeros((n, LANES - HEAD_DIM), v.dtype).at[:, 0].set(1)
    return jnp.concatenate([v[:, :HEAD_DIM], e, v[:, HEAD_DIM:], e], axis=1)


def _expand_heads(v):
    return jnp.repeat(v, SSM_HEAD_DIM)[None, :]


def kernel(x_prompt, x_sample, cache_k, cache_v, cache_kidx, state_ssm, state_ssm_conv, state_lru,
           state_lru_conv, norm_mix, norm_ffn, norm_final, w_in, ssm_conv_w, ssm_conv_b, ssm_dt_bias,
           ssm_a_log, ssm_d, ssm_norm, lru_conv_w, lru_conv_b, lru_wa, lru_ba, lru_wx, lru_bx, lru_lambda,
           w_att_out, w_ssm_out, w_lru_out, w_o, ffn_w1, ffn_w3, ffn_w2, moe_router, moe_w1, moe_w3, moe_w2):
    pb, pl_len, d = x_prompt.shape
    sb, sl_len, _ = x_sample.shape
    depth = w_in.shape[0]
    past = cache_k.shape[2]
    n_p = pb * pl_len
    n_s = sb * sl_len
    x = jnp.concatenate([x_prompt.reshape(n_p, d), x_sample.reshape(n_s, d)], axis=0)

    topk_p = min(TOPK_MAX, pl_len // 4)
    s_tot = past + sl_len
    topk_s = min(TOPK_MAX, s_tot // 4)
    kb_s = ATT_KB
    s_pad = -(-s_tot // kb_s) * kb_s

    cos_p, sin_p = _rope_tables(jnp.arange(pl_len))
    cos_s, sin_s = _rope_tables(past + jnp.arange(sl_len))

    groups = (
        dict(row0=0, nb=pb, length=pl_len),
        dict(row0=n_p, nb=sb, length=sl_len),
    )
    tl_p, tl_s = SEQ_TL, sl_len

    collected = [[[] for _ in range(7)] for _ in range(2)]
    final_fused = False
    for layer in range(depth):
        proj = _norm_matmul(x, norm_mix[layer][None, :], _pack_w_in(w_in[layer]), tm=IN_PROJ_TM, tn=IN_PROJ_TN)

        ssm_w = (ssm_conv_w[layer][:, :MIX], ssm_conv_w[layer][:, MIX:],
                 ssm_conv_b[layer][None, :MIX], ssm_conv_b[layer][None, MIX:],
                 _expand_heads(ssm_dt_bias[layer]), _expand_heads(ssm_a_log[layer]),
                 _expand_heads(ssm_d[layer]), ssm_norm[layer][None, :])
        lru_w = (lru_conv_w[layer], lru_conv_b[layer][None, :],
                 _block_diag(lru_wa[layer]).astype(BF16), lru_ba[layer][None, :],
                 _block_diag(lru_wx[layer]).astype(BF16), lru_bx[layer][None, :],
                 lru_lambda[layer][None, :])

        branch = [[], [], []]
        for gi, grp in enumerate(groups):
            nb, length = grp['nb'], grp['length']
            if gi == 0:
                tl, cos, sin = tl_p, cos_p, sin_p
                cx0 = jnp.zeros((nb, 8, MIX), F32)
                cbc0 = jnp.zeros((nb, 8, SSM_BC), F32)
                s0 = jnp.zeros((nb, 2 * SSM_STATE, MIX), F32)
                lc0 = jnp.zeros((nb, 8, MIX), F32)
                lh0 = jnp.zeros((nb, 1, MIX), F32)
            else:
                tl, cos, sin = tl_s, cos_s, sin_s
                conv0 = _pad_rows8(state_ssm_conv[layer])
                cx0, cbc0 = conv0[:, :, :MIX], conv0[:, :, MIX:]
                s0 = _state_to_s2(state_ssm[layer])
                lc0 = _pad_rows8(state_lru_conv[layer])
                lh0 = state_lru[layer][:, None, :]

            q_r, qi_r, k_r, v_r, ki_r, k_b, v_b, ki_b = _rope_call(proj, cos, sin, tl=tl, **grp)
            if gi == 0:
                att = _attn_call(q_r, qi_r, proj, k_b, _with_ones_column(v_b), ki_b, s_pad=length,
                                 s_valid=length, q_off=0, tq=ATT_TQ, kb=ATT_KB, topk=topk_p, **grp)
            else:
                def cat(cache, new, width):
                    c = cache.reshape(nb, past, width).astype(BF16)
                    a = jnp.concatenate([c, new.reshape(nb, length, width)], axis=1)
                    a = jnp.pad(a, ((0, 0), (0, s_pad - s_tot), (0, 0)))
                    return a.reshape(nb * s_pad, width)
                att = _attn_call(q_r, qi_r, proj, cat(cache_k[layer], k_b, 128),
                                 _with_ones_column(cat(cache_v[layer], v_b, 128)),
                                 cat(cache_kidx[layer], ki_b, IDX_DIM), s_pad=s_pad, s_valid=s_tot,
                                 q_off=past, tq=length, kb=kb_s, topk=topk_s, **grp)
            y_ssm, s_t, cx_t, cbc_t = _ssm_call(proj, cx0, cbc0, s0, ssm_w, tl=tl, **grp)
            y_lru, lh_t, lc_t = _lru_call(proj, lc0, lh0, lru_w, tl=tl, **grp)
            branch[0].append(att)
            branch[1].append(y_ssm)
            branch[2].append(y_lru)

            st = (k_r.reshape(nb, length, KV_HEADS, HEAD_DIM), v_r.reshape(nb, length, KV_HEADS, HEAD_DIM),
                  ki_r.reshape(nb, length, IDX_DIM), _s2_to_state(s_t),
                  jnp.concatenate([cx_t[:, 5:], cbc_t[:, 5:]], axis=2), lh_t[:, 0], lc_t[:, 5:])
            for lst, s in zip(collected[gi], st):
                lst.append(s)

        x = _merge_call(x, [bl[0] for bl in branch], [bl[1] for bl in branch], proj,
                        w_att_out[layer].astype(BF16),
                        w_ssm_out[layer].astype(BF16), w_lru_out[layer].astype(BF16),
                        w_o[layer].astype(BF16), tm=MERGE_TM)
        jl = layer // 2
        gf = norm_ffn[layer][None, :]
        if layer % 2 == 0:
            x = _ffn_call(x, gf, ffn_w1[jl].astype(BF16), ffn_w3[jl].astype(BF16),
                          ffn_w2[jl].astype(BF16), tm=FFN_TM, tf=FFN_TF)
        else:
            wr = jnp.pad(moe_router[jl], ((0, 0), (0, LANES - N_EXPERTS)))
            gate, xn, slot_tok, counts = _router_call(x, gf, wr, tm=ROW_TM, moe_tm=MOE_TM)
            nch, slot_exp = _moe_routing_tables(slot_tok, counts, tm=MOE_TM, ch=MOE_CH)
            yc = _moe_expert_call(nch, slot_exp, xn, moe_w1[jl].astype(BF16), moe_w3[jl].astype(BF16),
                                  moe_w2[jl].astype(BF16), tm=MOE_TM, tf=FFN_TF, ch=MOE_CH)
            final_fused = layer == depth - 1
            x = _moe_combine_call(nch, x, slot_tok, gate, yc, norm_final[None, :], tm=MOE_TM,
                                  ch=MOE_CH, final_norm=final_fused)

    y = x if final_fused else _norm_call(x, norm_final[None, :], tm=ROW_TM)
    y_prompt = y[:n_p].reshape(pb, pl_len, d)
    y_sample = y[n_p:].reshape(sb, sl_len, d)
    p_states = [jnp.stack(lst, axis=0) for lst in collected[0]]
    s_states = [jnp.stack(lst, axis=0) for lst in collected[1]]
    return (y_prompt, y_sample, *p_states, *s_states)
```

```python
import functools

import jax
import jax.numpy as jnp
import numpy as np
from jax import lax
from jax.experimental import pallas as pl
from jax.experimental.pallas import tpu as pltpu

F32 = jnp.float32
BF16 = jnp.bfloat16
I32 = jnp.int32

CHUNK = 64
HEAD_DIM = 64
ATT_HEADS = 8
KV_HEADS = 2
IDX_HEADS = 4
IDX_DIM = 64
TOPK_MAX = 256
ROPE_THETA = 10000.0
MIX = 512
SSM_HEADS = 8
SSM_HEAD_DIM = 64
SSM_GROUPS = 2
SSM_STATE = 64
SSM_BC = 2 * SSM_GROUPS * SSM_STATE
LRU_C = 8.0
N_EXPERTS = 8
NORM_EPS = 1e-6

LANES = 128
VMEM_LIMIT = 56 * 1024 * 1024

IN_PROJ_TM, IN_PROJ_TN = 1536, 1024
MERGE_TM = 512
FFN_TM, FFN_TF = 768, 1408
ROW_TM = 512
SEQ_TL = 512
ATT_TQ, ATT_KB = 256, 512
MOE_TM = 1536
MOE_CH = 448

COL_Q, COL_Z, COL_XL, COL_GL, COL_GATES = 0, 512, 1024, 1536, 2048
COL_XS, COL_DT, COL_BC, COL_QI, COL_K, COL_V, COL_KIWI = 5120, 5632, 6144, 6400, 6656, 6784, 6912
PROJ_W = 7168

LOG2_E = 1.4426950408889634
SHIFT_MARGIN = 1.01
ROWSUM_FLOOR = 2.0 ** -60
NEG_BIG = -1e30
KEY_NEG_INF = -2139095041
INT_MAX = 2147483647
INT_MIN = -2147483648


def _cparams(sem):
    return pltpu.CompilerParams(dimension_semantics=sem, vmem_limit_bytes=VMEM_LIMIT)


def _rms(x, g):
    ms = jnp.mean(x * x, axis=-1, keepdims=True)
    return x * lax.rsqrt(ms + NORM_EPS) * g


def _softplus(x):
    return jnp.maximum(x, 0.0) + jnp.log1p(jnp.exp(-jnp.abs(x)))


def _silu(x):
    return x * jax.nn.sigmoid(x)


def _norm_matmul_kernel(x_ref, g_ref, w_ref, o_ref, xn_ref):
    @pl.when(pl.program_id(1) == 0)
    def _():
        xn_ref[...] = _rms(x_ref[...], g_ref[...]).astype(BF16)

    o_ref[...] = jnp.dot(xn_ref[...], w_ref[...], preferred_element_type=F32)


def _norm_matmul(x, g, w, *, tm, tn):
    n, d = x.shape
    c = w.shape[1]
    return pl.pallas_call(
        _norm_matmul_kernel,
        grid=(n // tm, c // tn),
        in_specs=[pl.BlockSpec((tm, d), lambda i, j: (i, 0)),
                  pl.BlockSpec((1, d), lambda i, j: (0, 0)),
                  pl.BlockSpec((d, tn), lambda i, j: (0, j))],
        out_specs=pl.BlockSpec((tm, tn), lambda i, j: (i, j)),
        out_shape=jax.ShapeDtypeStruct((n, c), F32),
        scratch_shapes=[pltpu.VMEM((tm, d), BF16)],
        compiler_params=_cparams(("parallel", "arbitrary")),
        name="norm_in_proj",
    )(x, g, w)


def _rope_apply(x, cos, sin_signed, first_half):
    w = x.shape[1]
    reps = w // LANES
    if reps > 1:
        cos = jnp.concatenate([cos] * reps, axis=1)
        sin_signed = jnp.concatenate([sin_signed] * reps, axis=1)
        first_half = jnp.concatenate([first_half] * reps, axis=1)
    up = pltpu.roll(x, w - 32, axis=1)
    dn = pltpu.roll(x, 32, axis=1)
    return x * cos + jnp.where(first_half, up, dn) * sin_signed


def _rope_kernel(q_ref, qi_ref, k_ref, v_ref, kiwi_ref, cos_ref, sin_ref,
                 qo_ref, qio_ref, ko_ref, vo_ref, kio_ref, kbo_ref, vbo_ref, kibo_ref):
    cos = cos_ref[...]
    sin = sin_ref[...]
    lane = lax.broadcasted_iota(I32, cos.shape, 1)
    first_half = (lane % 64) < 32
    qo_ref[...] = _rope_apply(q_ref[...], cos, sin, first_half).astype(BF16)
    qio_ref[...] = _rope_apply(qi_ref[...], cos, sin, first_half).astype(BF16)
    k = _rope_apply(k_ref[...], cos, sin, first_half)
    ko_ref[...] = k
    kbo_ref[...] = k.astype(BF16)
    v = v_ref[...]
    vo_ref[...] = v
    vbo_ref[...] = v.astype(BF16)
    ki = _rope_apply(kiwi_ref[...], cos, sin, first_half)[:, :IDX_DIM]
    kio_ref[...] = ki
    kibo_ref[...] = ki.astype(BF16)


def _rope_call(proj, cos, sin, *, row0, nb, length, tl):
    nt = length // tl
    rb0 = row0 // tl
    n = nb * length

    def rows(b, j):
        return rb0 + b * nt + j

    def pspec(width, col):
        return pl.BlockSpec((tl, width), lambda b, j: (rows(b, j), col // width))

    def ospec(width):
        return pl.BlockSpec((tl, width), lambda b, j: (b * nt + j, 0))

    tspec = pl.BlockSpec((tl, LANES), lambda b, j: (j, 0))
    return pl.pallas_call(
        _rope_kernel,
        grid=(nb, nt),
        in_specs=[pspec(512, COL_Q), pspec(256, COL_QI), pspec(128, COL_K), pspec(128, COL_V),
                  pspec(128, COL_KIWI), tspec, tspec],
        out_specs=[ospec(512), ospec(256), ospec(128), ospec(128), ospec(IDX_DIM),
                   ospec(128), ospec(128), ospec(IDX_DIM)],
        out_shape=[jax.ShapeDtypeStruct((n, 512), BF16), jax.ShapeDtypeStruct((n, 256), BF16),
                   jax.ShapeDtypeStruct((n, 128), F32), jax.ShapeDtypeStruct((n, 128), F32),
                   jax.ShapeDtypeStruct((n, IDX_DIM), F32),
                   jax.ShapeDtypeStruct((n, 128), BF16), jax.ShapeDtypeStruct((n, 128), BF16),
                   jax.ShapeDtypeStruct((n, IDX_DIM), BF16)],
        compiler_params=_cparams(("parallel", "parallel")),
        name="rope",
    )(proj, proj, proj, proj, proj, cos, sin)


def _attn_kernel(q_ref, qi_ref, kiwi_ref, k_ref, v_ref, ki_ref, o_ref, key_ref, bias_ref,
                 plane_ref, act_ref, m_ref, acc_ref, knb_ref, *, tq, kb, s_valid, q_off, topk):
    i = pl.program_id(1)
    t0 = i * tq
    q_last = q_off + t0 + tq - 1
    n_adm = jnp.minimum((q_last // CHUNK + 1) * CHUNK, s_valid)
    nkb = (n_adm + kb - 1) // kb
    nsub = kb // LANES

    wi = kiwi_ref[:, IDX_DIM:IDX_DIM + IDX_HEADS]
    q_chunk = (q_off + t0 + lax.broadcasted_iota(I32, (tq, 1), 0)) // CHUNK
    nt_dims = (((1,), (1,)), ((), ()))

    @pl.when(i == 0)
    def _():
        if plane_ref.shape[0] > 32:
            plane_ref[32:] = jnp.zeros((plane_ref.shape[0] - 32,) + plane_ref.shape[1:], I32)
        for j in range(knb_ref.shape[0]):
            kf = k_ref[j * kb:(j + 1) * kb, :].astype(F32)
            ksq = jnp.sum(kf * kf, axis=1, keepdims=True)
            knb_ref[j] = jnp.broadcast_to(jnp.max(ksq, axis=0, keepdims=True), knb_ref.shape[1:])

    def kmax_body(j, m):
        return jnp.maximum(m, knb_ref[j])

    kmax2 = lax.fori_loop(0, nkb, kmax_body, jnp.zeros(knb_ref.shape[1:], F32))[0:1, 0:1]

    wib = [jnp.broadcast_to(wi[:, h:h + 1], (tq, LANES)) for h in range(IDX_HEADS)]

    def score_block(j, masked):
        off = pl.multiple_of(j * kb, kb)
        ki_blk = ki_ref[pl.ds(off, kb), :]
        sc = None
        for h in range(IDX_HEADS):
            qh = qi_ref[:, h * IDX_DIM:(h + 1) * IDX_DIM]
            s = lax.dot_general(qh, ki_blk, nt_dims, preferred_element_type=F32)
            wb = wib[h] if nsub == 1 else jnp.concatenate([wib[h]] * nsub, axis=1)
            term = jnp.maximum(s, 0.0) * wb
            sc = term if sc is None else sc + term
        sc = jnp.where(sc == 0.0, 0.0, sc)
        if masked:
            kpos = off + lax.broadcasted_iota(I32, (1, kb), 1)
            adm = ((kpos // CHUNK) <= q_chunk) & (kpos < s_valid)
            sc = jnp.where(adm, sc, -jnp.inf)
        bits = pltpu.bitcast(sc, I32)
        key_ref[j] = bits ^ ((bits >> 31) & INT_MAX)

    nfull = jnp.minimum(((q_off + t0) // CHUNK + 1) * CHUNK, s_valid) // kb

    def score_full(j, carry):
        score_block(j, False)
        return carry

    def score_edge(j, carry):
        score_block(j, True)
        return carry

    lax.fori_loop(0, nfull, score_full, 0)
    lax.fori_loop(nfull, nkb, score_edge, 0)

    bpg = 32 // nsub
    ng = (nkb + bpg - 1) // bpg
    nslab = tq // 8

    def fill_body(j, carry):
        key_ref[j] = jnp.full((tq, kb), INT_MIN, I32)
        return carry

    lax.fori_loop(nkb, ng * bpg, fill_body, 0)

    def transpose_body(idx, carry):
        g = idx // nslab
        r0 = pl.multiple_of((idx % nslab) * 8, 8)
        xs = [key_ref[g * bpg + t // nsub, pl.ds(r0, 8), (t % nsub) * LANES:(t % nsub + 1) * LANES]
              ^ INT_MIN for t in range(32)]
        j, m = 16, 0x0000FFFF
        while j:
            k = 0
            while k < 32:
                t = (xs[k] ^ lax.shift_right_logical(xs[k + j], j)) & m
                xs[k] = xs[k] ^ t
                xs[k + j] = xs[k + j] ^ (t << j)
                k = (k + j + 1) & ~j
            j >>= 1
            m = (m ^ (m << j)) & 0xFFFFFFFF
        for b in range(32):
            plane_ref[g * 32 + b, pl.ds(r0, 8), :] = xs[b]
        return carry

    lax.fori_loop(0, ng * nslab, transpose_body, 0)

    ng_max = act_ref.shape[0]
    for g in range(ng_max):
        act_ref[g] = jnp.broadcast_to(jnp.where(g < ng, -1, 0), (tq, LANES)).astype(I32)

    nhalf = 2 if tq % 16 == 0 and tq >= 128 else 1
    hrows = tq // nhalf

    def bit_body(step, carry):
        out = []
        for hf in range(nhalf):
            rem, thr_u = carry[2 * hf], carry[2 * hf + 1]
            rs = slice(hf * hrows, (hf + 1) * hrows)
            n1l = jnp.zeros((hrows, LANES), I32)
            for g in range(ng_max):
                n1l = n1l + lax.population_count(act_ref[g, rs, :] & plane_ref[g * 32 + step, rs, :])
            n1 = jnp.sum(n1l, axis=1, keepdims=True)
            take = n1 >= rem
            rem = jnp.where(take, rem, rem - n1)
            thr_u = jnp.where(take, thr_u | jnp.left_shift(jnp.int32(1), 31 - step), thr_u)
            for g in range(ng_max):
                a = act_ref[g, rs, :]
                w = a & plane_ref[g * 32 + step, rs, :]
                act_ref[g, rs, :] = jnp.where(take, w, a ^ w)
            out += [rem, thr_u]
        return tuple(out)

    init = (jnp.full((hrows, 1), topk, I32), jnp.zeros((hrows, 1), I32)) * nhalf
    fin = lax.fori_loop(0, 32, bit_body, init)
    rem = fin[0] if nhalf == 1 else jnp.concatenate(fin[0::2], axis=0)
    thr_u = fin[1] if nhalf == 1 else jnp.concatenate(fin[1::2], axis=0)
    thr = thr_u ^ INT_MIN

    eq_l = jnp.zeros((tq, LANES), I32)
    for g in range(ng_max):
        eq_l = eq_l + lax.population_count(act_ref[g])
    eq_cnt = jnp.sum(eq_l, axis=1, keepdims=True)
    has_ties = jnp.max(eq_cnt - rem) > 0

    @pl.when(jnp.logical_not(has_ties))
    def _():
        thr_eff = jnp.maximum(thr, KEY_NEG_INF + 1)

        def body(j, carry):
            bias_ref[j] = jnp.where(key_ref[j] >= thr_eff, 0.0, NEG_BIG)
            return carry

        lax.fori_loop(0, nkb, body, 0)

    @pl.when(has_ties)
    def _():
        need = rem.astype(F32)
        thr_ok = thr > KEY_NEG_INF
        r = lax.broadcasted_iota(I32, (kb, kb), 0)
        c = lax.broadcasted_iota(I32, (kb, kb), 1)
        upper = jnp.where(r < c, 1.0, 0.0).astype(BF16)

        def body(j, carry):
            key = key_ref[j]
            gt = key > thr
            eq = (key == thr) & thr_ok
            eqf = jnp.where(eq, 1.0, 0.0)
            rank = jnp.dot(eqf.astype(BF16), upper, preferred_element_type=F32) + carry
            sel = gt | (eq & (rank < need))
            bias_ref[j] = jnp.where(sel, 0.0, NEG_BIG)
            return carry + jnp.sum(eqf, axis=1, keepdims=True)

        lax.fori_loop(0, nkb, body, jnp.zeros((tq, 1), F32))

    hpg = ATT_HEADS // KV_HEADS

    def logits(j, h, kblk):
        qh = q_ref[:, h * HEAD_DIM:(h + 1) * HEAD_DIM]
        s = lax.dot_general(qh, kblk, nt_dims, preferred_element_type=F32)
        return s + bias_ref[j]

    def exp_and_values():
        acc_ref[...] = jnp.zeros(acc_ref.shape, F32)
        qg = [jnp.concatenate([q_ref[:, h * HEAD_DIM:(h + 1) * HEAD_DIM]
                               for h in range(g * hpg, (g + 1) * hpg)], axis=0)
              for g in range(KV_HEADS)]

        def pv_body(j, carry):
            off = pl.multiple_of(j * kb, kb)
            bias = bias_ref[j]
            for g in range(KV_HEADS):
                kblk = k_ref[pl.ds(off, kb), g * HEAD_DIM:(g + 1) * HEAD_DIM]
                vblk = v_ref[pl.ds(off, kb), g * LANES:(g + 1) * LANES]
                s_all = lax.dot_general(qg[g], kblk, nt_dims, preferred_element_type=F32)
                ps_all = []
                for hh in range(hpg):
                    s = s_all[hh * tq:(hh + 1) * tq] + bias
                    mb = m_ref[g * hpg + hh]
                    ps = [jnp.exp2(s[:, c * LANES:(c + 1) * LANES] - mb) for c in range(nsub)]
                    p = jnp.concatenate(ps, axis=1) if nsub > 1 else ps[0]
                    ps_all.append(p.astype(BF16))
                out = jnp.dot(jnp.concatenate(ps_all, axis=0), vblk, preferred_element_type=F32)
                for hh in range(hpg):
                    acc_ref[g * hpg + hh] += out[hh * tq:(hh + 1) * tq]
            return carry

        lax.fori_loop(0, nkb, pv_body, 0)

    qf = q_ref[...].astype(F32)
    for h in range(ATT_HEADS):
        qh = qf[:, h * HEAD_DIM:(h + 1) * HEAD_DIM]
        bound = jnp.sqrt(jnp.sum(qh * qh, axis=1, keepdims=True) * kmax2) * SHIFT_MARGIN
        m_ref[h] = jnp.broadcast_to(bound, (tq, LANES))
    exp_and_values()
    lmin = jnp.min(acc_ref[0][:, HEAD_DIM:HEAD_DIM + 1])
    for h in range(1, ATT_HEADS):
        lmin = jnp.minimum(lmin, jnp.min(acc_ref[h][:, HEAD_DIM:HEAD_DIM + 1]))

    @pl.when(jnp.logical_not(lmin >= ROWSUM_FLOOR))
    def _():
        m_ref[...] = jnp.full(m_ref.shape, NEG_BIG, F32)

        def max_body(j, carry):
            off = pl.multiple_of(j * kb, kb)
            for g in range(KV_HEADS):
                kblk = k_ref[pl.ds(off, kb), g * HEAD_DIM:(g + 1) * HEAD_DIM]
                for hh in range(hpg):
                    h = g * hpg + hh
                    s = logits(j, h, kblk)
                    mt = s[:, 0:LANES]
                    for c in range(1, nsub):
                        mt = jnp.maximum(mt, s[:, c * LANES:(c + 1) * LANES])
                    m_ref[h] = jnp.maximum(m_ref[h], mt)
            return carry

        lax.fori_loop(0, nkb, max_body, 0)
        for h in range(ATT_HEADS):
            m_ref[h] = jnp.broadcast_to(jnp.max(m_ref[h], axis=1, keepdims=True), (tq, LANES))
        exp_and_values()

    for h in range(ATT_HEADS):
        acc = acc_ref[h]
        o_ref[:, h * HEAD_DIM:(h + 1) * HEAD_DIM] = (
            acc[:, :HEAD_DIM] / acc[:, HEAD_DIM:HEAD_DIM + 1]).astype(BF16)


def _attn_call(q, qi, proj, k, v, ki, *, row0, nb, length, s_pad, s_valid, q_off, tq, kb, topk):
    nq = length // tq
    rb0 = row0 // tq
    nkb_max = s_pad // kb
    bpg = 32 // (kb // LANES)
    ng_max = -(-nkb_max // bpg)
    kern = functools.partial(_attn_kernel, tq=tq, kb=kb, s_valid=s_valid, q_off=q_off, topk=topk)
    return pl.pallas_call(
        kern,
        grid=(nb, nq),
        in_specs=[pl.BlockSpec((tq, 512), lambda b, i: (b * nq + i, 0)),
                  pl.BlockSpec((tq, 256), lambda b, i: (b * nq + i, 0)),
                  pl.BlockSpec((tq, 128), lambda b, i: (rb0 + b * nq + i, COL_KIWI // 128)),
                  pl.BlockSpec((s_pad, 128), lambda b, i: (b, 0)),
                  pl.BlockSpec((s_pad, 2 * LANES), lambda b, i: (b, 0)),
                  pl.BlockSpec((s_pad, IDX_DIM), lambda b, i: (b, 0))],
        out_specs=pl.BlockSpec((tq, 512), lambda b, i: (b * nq + i, 0)),
        out_shape=jax.ShapeDtypeStruct((nb * length, 512), BF16),
        scratch_shapes=[pltpu.VMEM((ng_max * bpg, tq, kb), I32), pltpu.VMEM((nkb_max, tq, kb), F32),
                        pltpu.VMEM((ng_max * 32, tq, LANES), I32), pltpu.VMEM((ng_max, tq, LANES), I32),
                        pltpu.VMEM((ATT_HEADS, tq, LANES), F32), pltpu.VMEM((ATT_HEADS, tq, LANES), F32),
                        pltpu.VMEM((nkb_max, 8, LANES), F32)],
        compiler_params=_cparams(("parallel", "arbitrary")),
        name="dsa_attention",
    )(q, qi, proj, k, v, ki)


def _ssm_kernel(z_ref, xs_ref, dt_ref, bc_ref, cx0_ref, cbc0_ref, s0_ref,
                cwx_ref, cwbc_ref, cbx_ref, cbbc_ref, dtb_ref, alog_ref, dsk_ref, nw_ref,
                y_ref, sT_ref, cxT_ref, cbcT_ref,
                xpx_ref, xpbc_ref, xc_ref, bcc_ref, dtc_ref, ypre_ref, st_ref,
                *, t_in, t_pad):
    j = pl.program_id(1)
    nj = pl.num_programs(1)
    q = CHUNK

    @pl.when(j == 0)
    def _():
        xpx_ref[0:8, :] = cx0_ref[...]
        xpbc_ref[0:8, :] = cbc0_ref[...]
        st_ref[...] = s0_ref[...]

    xpx_ref[8:8 + t_in, :] = xs_ref[...]
    xpbc_ref[8:8 + t_in, :] = bc_ref[...]

    def conv(xp_ref, w_ref, b_ref):
        y = b_ref[...]
        for tap in range(4):
            y = y + xp_ref[5 + tap:5 + tap + t_in, :] * w_ref[tap:tap + 1, :]
        return _silu(y)

    if t_pad > t_in:
        xc_ref[...] = jnp.zeros_like(xc_ref)
        bcc_ref[...] = jnp.zeros_like(bcc_ref)
        dtc_ref[...] = jnp.zeros_like(dtc_ref)
    xc_ref[0:t_in, :] = conv(xpx_ref, cwx_ref, cbx_ref)
    bcc_ref[0:t_in, :] = conv(xpbc_ref, cwbc_ref, cbbc_ref)
    dtc_ref[0:t_in, :] = _softplus(dt_ref[...] + dtb_ref[...])

    last_x = xpx_ref[t_in:t_in + 8, :]
    last_bc = xpbc_ref[t_in:t_in + 8, :]
    xpx_ref[0:8, :] = last_x
    xpbc_ref[0:8, :] = last_bc

    a_neg = -jnp.exp(alog_ref[...])
    li = lax.broadcasted_iota(I32, (q, q), 0)
    si = lax.broadcasted_iota(I32, (q, q), 1)
    tri = jnp.where(si <= li, 1.0, 0.0).astype(BF16)
    ones = jnp.ones((q, q), BF16)
    lane = lax.broadcasted_iota(I32, (q, MIX), 1)
    row = lax.broadcasted_iota(I32, (q, MIX), 0)
    s_of_lane = lane % q
    mask_t_le_s = jnp.where(row <= s_of_lane, 1.0, 0.0)
    causal = s_of_lane <= row
    rg = lax.broadcasted_iota(I32, (SSM_HEADS * q, 2 * SSM_STATE), 0) // (q * SSM_HEADS // SSM_GROUPS)
    cg = lax.broadcasted_iota(I32, (SSM_HEADS * q, 2 * SSM_STATE), 1) // SSM_STATE
    gmask = rg == cg
    rh = lax.broadcasted_iota(I32, (SSM_HEADS * q, MIX), 0) // q
    ch = lax.broadcasted_iota(I32, (SSM_HEADS * q, MIX), 1) // SSM_HEAD_DIM
    hmask = rh == ch
    r2 = lax.broadcasted_iota(I32, (2 * SSM_STATE, MIX), 0) // SSM_STATE
    c2 = lax.broadcasted_iota(I32, (2 * SSM_STATE, MIX), 1) // (MIX // SSM_GROUPS)
    g2mask = r2 == c2
    nt_dims = (((1,), (1,)), ((), ()))
    tn_dims = (((0,), (0,)), ((), ()))

    def sum_rows(sel01, a):
        a_hi = a.astype(BF16)
        r1 = a - a_hi.astype(F32)
        a_mid = r1.astype(BF16)
        a_lo = (r1 - a_mid.astype(F32)).astype(BF16)
        return (jnp.dot(sel01, a_hi, preferred_element_type=F32)
                + jnp.dot(sel01, a_mid, preferred_element_type=F32)
                + jnp.dot(sel01, a_lo, preferred_element_type=F32))

    def chunk_body(c, carry):
        r0 = pl.multiple_of(c * q, q)
        xs = xc_ref[pl.ds(r0, q), :]
        dt = dtc_ref[pl.ds(r0, q), :]
        bmat = bcc_ref[pl.ds(r0, q), 0:2 * SSM_STATE]
        cmat = bcc_ref[pl.ds(r0, q), 2 * SSM_STATE:4 * SSM_STATE]
        a = dt * a_neg
        xdt = xs * dt
        acum = sum_rows(tri, a)
        rowt = sum_rows(ones, a * mask_t_le_s)
        decay_in = jnp.where(causal, jnp.exp(acum - rowt), 0.0)
        bexp = jnp.where(gmask, jnp.concatenate([bmat] * SSM_HEADS, axis=0), 0.0)
        cb = lax.dot_general(cmat.astype(BF16), bexp.astype(BF16), nt_dims,
                             preferred_element_type=F32)
        m = (cb * decay_in).astype(BF16)
        bdx = jnp.where(hmask, jnp.concatenate([xdt] * SSM_HEADS, axis=0), 0.0).astype(BF16)
        y_diag = jnp.dot(m, bdx, preferred_element_type=F32)
        st = st_ref[...]
        y_off = jnp.exp(acum) * jnp.dot(cmat.astype(BF16), st.astype(BF16),
                                        preferred_element_type=F32)
        a_end = acum[q - 1:q, :]
        xd = (xdt * jnp.exp(a_end - acum)).astype(BF16)
        upd = lax.dot_general(bmat.astype(BF16), xd, tn_dims, preferred_element_type=F32)
        st_ref[...] = jnp.exp(a_end) * st + jnp.where(g2mask, upd, 0.0)
        ypre_ref[pl.ds(r0, q), :] = y_diag + y_off
        return carry

    lax.fori_loop(0, t_pad // q, chunk_body, 0)

    xs = xc_ref[0:t_in, :]
    y = ypre_ref[0:t_in, :] + dsk_ref[...] * xs
    y = y * _silu(z_ref[...])
    half = MIX // SSM_GROUPS
    parts = []
    for g in range(SSM_GROUPS):
        yg = y[:, g * half:(g + 1) * half]
        parts.append(yg * lax.rsqrt(jnp.mean(yg * yg, axis=-1, keepdims=True) + NORM_EPS))
    y = jnp.concatenate(parts, axis=1) * nw_ref[...]
    y_ref[...] = y.astype(BF16)

    @pl.when(j == nj - 1)
    def _():
        sT_ref[...] = st_ref[...]
        cxT_ref[...] = last_x
        cbcT_ref[...] = last_bc


def _ssm_call(proj, cx0, cbc0, s0, wts, *, row0, nb, length, tl):
    nt = length // tl
    rb0 = row0 // tl
    t_pad = -(-tl // CHUNK) * CHUNK
    kern = functools.partial(_ssm_kernel, t_in=tl, t_pad=t_pad)

    def pspec(width, col):
        return pl.BlockSpec((tl, width), lambda b, j: (rb0 + b * nt + j, col // width))

    def bspec(r, c):
        return pl.BlockSpec((None, r, c), lambda b, j: (b, 0, 0))

    def wspec(r, c):
        return pl.BlockSpec((r, c), lambda b, j: (0, 0))

    return pl.pallas_call(
        kern,
        grid=(nb, nt),
        in_specs=[pspec(512, COL_Z), pspec(512, COL_XS), pspec(512, COL_DT), pspec(256, COL_BC),
                  bspec(8, MIX), bspec(8, SSM_BC), bspec(2 * SSM_STATE, MIX),
                  wspec(4, MIX), wspec(4, SSM_BC), wspec(1, MIX), wspec(1, SSM_BC),
                  wspec(1, MIX), wspec(1, MIX), wspec(1, MIX), wspec(1, MIX)],
        out_specs=[pl.BlockSpec((tl, MIX), lambda b, j: (b * nt + j, 0)),
                   bspec(2 * SSM_STATE, MIX), bspec(8, MIX), bspec(8, SSM_BC)],
        out_shape=[jax.ShapeDtypeStruct((nb * length, MIX), BF16),
                   jax.ShapeDtypeStruct((nb, 2 * SSM_STATE, MIX), F32),
                   jax.ShapeDtypeStruct((nb, 8, MIX), F32),
                   jax.ShapeDtypeStruct((nb, 8, SSM_BC), F32)],
        scratch_shapes=[pltpu.VMEM((tl + 8, MIX), F32), pltpu.VMEM((tl + 8, SSM_BC), F32),
                        pltpu.VMEM((t_pad, MIX), F32), pltpu.VMEM((t_pad, SSM_BC), F32),
                        pltpu.VMEM((t_pad, MIX), F32), pltpu.VMEM((t_pad, MIX), F32),
                        pltpu.VMEM((2 * SSM_STATE, MIX), F32)],
        compiler_params=_cparams(("parallel", "arbitrary")),
        name="ssd_branch",
    )(proj, proj, proj, proj, cx0, cbc0, s0, *wts)


def _lru_kernel(xl_ref, gl_ref, c0_ref, h0_ref, cw_ref, cb_ref, wa_ref, ba_ref, wx_ref, bx_ref,
                lam_ref, y_ref, hT_ref, cT_ref, xp_ref, a_ref, u_ref, hs_ref, h_ref, *, tl):
    j = pl.program_id(1)
    nj = pl.num_programs(1)

    @pl.when(j == 0)
    def _():
        xp_ref[0:8, :] = c0_ref[...]
        h_ref[...] = h0_ref[...]

    xp_ref[8:8 + tl, :] = xl_ref[...]
    xc = cb_ref[...]
    for tap in range(4):
        xc = xc + xp_ref[5 + tap:5 + tap + tl, :] * cw_ref[tap:tap + 1, :]
    last = xp_ref[tl:tl + 8, :]
    xp_ref[0:8, :] = last

    xcb = xc.astype(BF16)
    r = jax.nn.sigmoid(jnp.dot(xcb, wa_ref[...], preferred_element_type=F32) + ba_ref[...])
    i = jax.nn.sigmoid(jnp.dot(xcb, wx_ref[...], preferred_element_type=F32) + bx_ref[...])
    log_a = (-LRU_C * _softplus(-lam_ref[...])) * r
    a = jnp.exp(log_a)
    mult = jnp.sqrt(-jnp.tanh(log_a) * (a * a + 1.0))
    a_ref[...] = a
    u_ref[...] = mult * (i * xc)

    row = lax.broadcasted_iota(I32, (8, MIX), 0)

    def step(g, h):
        r0 = pl.multiple_of(g * 8, 8)
        ga = a_ref[pl.ds(r0, 8), :]
        gu = u_ref[pl.ds(r0, 8), :]
        for d in (1, 2, 4):
            prev_a = jnp.where(row >= d, pltpu.roll(ga, d, axis=0), 1.0)
            prev_u = jnp.where(row >= d, pltpu.roll(gu, d, axis=0), 0.0)
            gu = ga * prev_u + gu
            ga = ga * prev_a
        hg = ga * h + gu
        hs_ref[pl.ds(r0, 8), :] = hg
        return hg[7:8, :]

    h = lax.fori_loop(0, tl // 8, step, h_ref[...], unroll=4)
    h_ref[...] = h
    y_ref[...] = (hs_ref[...] * jax.nn.gelu(gl_ref[...], approximate=True)).astype(BF16)

    @pl.when(j == nj - 1)
    def _():
        hT_ref[...] = h
        cT_ref[...] = last


def _lru_call(proj, c0, h0, wts, *, row0, nb, length, tl):
    nt = length // tl
    rb0 = row0 // tl
    kern = functools.partial(_lru_kernel, tl=tl)

    def pspec(col):
        return pl.BlockSpec((tl, MIX), lambda b, j: (rb0 + b * nt + j, col // MIX))

    def bspec(r):
        return pl.BlockSpec((None, r, MIX), lambda b, j: (b, 0, 0))

    def wspec(r):
        return pl.BlockSpec((r, MIX), lambda b, j: (0, 0))

    return pl.pallas_call(
        kern,
        grid=(nb, nt),
        in_specs=[pspec(COL_XL), pspec(COL_GL), bspec(8), bspec(1),
                  wspec(4), wspec(1), wspec(MIX), wspec(1), wspec(MIX), wspec(1), wspec(1)],
        out_specs=[pl.BlockSpec((tl, MIX), lambda b, j: (b * nt + j, 0)), bspec(1), bspec(8)],
        out_shape=[jax.ShapeDtypeStruct((nb * length, MIX), BF16),
                   jax.ShapeDtypeStruct((nb, 1, MIX), F32),
                   jax.ShapeDtypeStruct((nb, 8, MIX), F32)],
        scratch_shapes=[pltpu.VMEM((tl + 8, MIX), F32), pltpu.VMEM((tl, MIX), F32),
                        pltpu.VMEM((tl, MIX), F32), pltpu.VMEM((tl, MIX), F32),
                        pltpu.VMEM((1, MIX), F32)],
        compiler_params=_cparams(("parallel", "arbitrary")),
        name="rglru_branch",
    )(proj, proj, c0, h0, *wts)


def _merge_kernel(x_ref, attp_ref, ssmp_ref, lrup_ref, atts_ref, ssms_ref, lrus_ref,
                  g0_ref, g1_ref, g2_ref, wa_ref, ws_ref, wl_ref, wo_ref, o_ref, *, prompt_tiles):
    is_prompt = pl.program_id(0) < prompt_tiles
    att = jnp.where(is_prompt, attp_ref[...], atts_ref[...])
    ssm = jnp.where(is_prompt, ssmp_ref[...], ssms_ref[...])
    lru = jnp.where(is_prompt, lrup_ref[...], lrus_ref[...])
    merged = (jax.nn.sigmoid(g0_ref[...]) * jnp.dot(att, wa_ref[...], preferred_element_type=F32)
              + jax.nn.sigmoid(g1_ref[...]) * jnp.dot(ssm, ws_ref[...], preferred_element_type=F32)
              + jax.nn.sigmoid(g2_ref[...]) * jnp.dot(lru, wl_ref[...], preferred_element_type=F32))
    o_ref[...] = x_ref[...] + jnp.dot(merged.astype(BF16), wo_ref[...], preferred_element_type=F32)


def _merge_call(x, branches_p, branches_s, proj, wa, ws, wl, wo, *, tm):
    n, d = x.shape
    n_p = branches_p[0].shape[0]
    n_s = branches_s[0].shape[0]
    assert n_p % tm == 0 and n_s % tm == 0 and n_p + n_s == n
    pt = n_p // tm
    st = n_s // tm

    def rspec(width):
        return pl.BlockSpec((tm, width), lambda i: (i, 0))

    pspec = pl.BlockSpec((tm, MIX), lambda i: (jnp.minimum(i, pt - 1), 0))
    sspec = pl.BlockSpec((tm, MIX), lambda i: (jnp.clip(i - pt, 0, st - 1), 0))

    def gspec(k):
        return pl.BlockSpec((tm, d), lambda i: (i, COL_GATES // d + k))

    def wspec(r):
        return pl.BlockSpec((r, d), lambda i: (0, 0))

    return pl.pallas_call(
        functools.partial(_merge_kernel, prompt_tiles=pt),
        grid=(n // tm,),
        in_specs=[rspec(d), pspec, pspec, pspec, sspec, sspec, sspec, gspec(0), gspec(1), gspec(2),
                  wspec(MIX), wspec(MIX), wspec(MIX), wspec(d)],
        out_specs=rspec(d),
        out_shape=jax.ShapeDtypeStruct((n, d), F32),
        compiler_params=_cparams(("parallel",)),
        name="branch_merge",
    )(x, *branches_p, *branches_s, proj, proj, proj, wa, ws, wl, wo)


def _ffn_kernel(x_ref, g_ref, w1_ref, w3_ref, w2_ref, o_ref, xn_ref, acc_ref):
    f = pl.program_id(1)

    @pl.when(f == 0)
    def _():
        xn_ref[...] = _rms(x_ref[...], g_ref[...]).astype(BF16)
        acc_ref[...] = jnp.zeros_like(acc_ref)

    xn = xn_ref[...]
    h1 = jnp.dot(xn, w1_ref[...], preferred_element_type=F32)
    h3 = jnp.dot(xn, w3_ref[...], preferred_element_type=F32)
    h = (_silu(h1) * h3).astype(BF16)
    acc_ref[...] += jnp.dot(h, w2_ref[...], preferred_element_type=F32)

    @pl.when(f == pl.num_programs(1) - 1)
    def _():
        o_ref[...] = x_ref[...] + acc_ref[...]


def _ffn_call(x, g, w1, w3, w2, *, tm, tf):
    n, d = x.shape
    dff = w1.shape[1]
    return pl.pallas_call(
        _ffn_kernel,
        grid=(n // tm, dff // tf),
        in_specs=[pl.BlockSpec((tm, d), lambda i, f: (i, 0)),
                  pl.BlockSpec((1, d), lambda i, f: (0, 0)),
                  pl.BlockSpec((d, tf), lambda i, f: (0, f)),
                  pl.BlockSpec((d, tf), lambda i, f: (0, f)),
                  pl.BlockSpec((tf, d), lambda i, f: (f, 0))],
        out_specs=pl.BlockSpec((tm, d), lambda i, f: (i, 0)),
        out_shape=jax.ShapeDtypeStruct((n, d), F32),
        scratch_shapes=[pltpu.VMEM((tm, d), BF16), pltpu.VMEM((tm, d), F32)],
        compiler_params=_cparams(("parallel", "arbitrary")),
        name="swiglu_ffn",
    )(x, g, w1, w3, w2)


def _router_kernel(x_ref, g_ref, wr_ref, gate_ref, xn_ref, slot_ref, cnt_ref, run_ref, *, sub):
    i = pl.program_id(0)
    xn = _rms(x_ref[...], g_ref[...])
    xn_ref[...] = xn.astype(BF16)
    logits = jnp.dot(xn, wr_ref[...], precision=lax.Precision.HIGHEST, preferred_element_type=F32)
    lane = lax.broadcasted_iota(I32, logits.shape, 1)
    logits = jnp.where(lane < N_EXPERTS, logits, -jnp.inf)
    m1 = jnp.max(logits, axis=1, keepdims=True)
    i1 = jnp.min(jnp.where(logits == m1, lane, LANES), axis=1, keepdims=True)
    rest = jnp.where(lane == i1, -jnp.inf, logits)
    m2 = jnp.max(rest, axis=1, keepdims=True)
    i2 = jnp.min(jnp.where(rest == m2, lane, LANES), axis=1, keepdims=True)
    e2 = jnp.exp(m2 - m1)
    den = 1.0 + e2
    gate = jnp.where(lane == i1, 1.0 / den, 0.0) + jnp.where(lane == i2, e2 / den, 0.0)
    gate_ref[...] = gate

    @pl.when(i % sub == 0)
    def _():
        run_ref[...] = jnp.zeros_like(run_ref)

    tm = gate.shape[0]
    routed = jnp.where(gate != 0.0, 1.0, 0.0)
    r = lax.broadcasted_iota(I32, (tm, tm), 0)
    c = lax.broadcasted_iota(I32, (tm, tm), 1)
    earlier = jnp.where(c < r, 1.0, 0.0).astype(BF16)
    rank = jnp.dot(earlier, routed.astype(BF16), preferred_element_type=F32) + run_ref[0:1, :]
    slot_ref[...] = jnp.where(routed > 0.0, rank, -1.0)
    total = run_ref[0:1, :] + jnp.sum(routed, axis=0, keepdims=True)
    run_ref[...] = jnp.broadcast_to(total, run_ref.shape)
    cnt_ref[...] = jnp.broadcast_to(total, cnt_ref.shape)


def _router_call(x, g, wr, *, tm, moe_tm):
    n, d = x.shape
    sub = moe_tm // tm
    return pl.pallas_call(
        functools.partial(_router_kernel, sub=sub),
        grid=(n // tm,),
        in_specs=[pl.BlockSpec((tm, d), lambda i: (i, 0)),
                  pl.BlockSpec((1, d), lambda i: (0, 0)),
                  pl.BlockSpec((d, LANES), lambda i: (0, 0))],
        out_specs=[pl.BlockSpec((tm, LANES), lambda i: (i, 0)), pl.BlockSpec((tm, d), lambda i: (i, 0)),
                   pl.BlockSpec((tm, LANES), lambda i: (i, 0)),
                   pl.BlockSpec((None, 8, LANES), lambda i: (i // sub, 0, 0))],
        out_shape=[jax.ShapeDtypeStruct((n, LANES), F32), jax.ShapeDtypeStruct((n, d), BF16),
                   jax.ShapeDtypeStruct((n, LANES), F32),
                   jax.ShapeDtypeStruct((n // moe_tm, 8, LANES), F32)],
        scratch_shapes=[pltpu.VMEM((8, LANES), F32)],
        compiler_params=_cparams(("arbitrary",)),
        name="moe_router",
    )(x, g, wr)


def _moe_expert_kernel(nch_ref, slot_ref, xn_ref, w1_ref, w3_ref, w2_ref, yc_ref, xg_ref, yacc_ref,
                       *, ch):
    e = pl.program_id(0)
    t = pl.program_id(1)
    f = pl.program_id(2)
    nch = nch_ref[t * pl.num_programs(0) + e]

    @pl.when(f == 0)
    def _():
        srow = slot_ref[pl.ds(e, 1), :]

        def gather(c, carry):
            rows = c * ch + lax.broadcasted_iota(I32, (ch, 1), 0)
            onehot = jnp.where(srow == rows, 1.0, 0.0).astype(BF16)
            xg_ref[c] = jnp.dot(onehot, xn_ref[...], preferred_element_type=F32).astype(BF16)
            yacc_ref[c] = jnp.zeros((ch, yacc_ref.shape[2]), F32)
            return carry

        lax.fori_loop(0, nch, gather, 0)

    def ffn(c, carry):
        xg = xg_ref[c]
        h1 = jnp.dot(xg, w1_ref[...], preferred_element_type=F32)
        h3 = jnp.dot(xg, w3_ref[...], preferred_element_type=F32)
        h = (_silu(h1) * h3).astype(BF16)
        yacc_ref[c] += jnp.dot(h, w2_ref[...], preferred_element_type=F32)
        return carry

    lax.fori_loop(0, nch, ffn, 0)

    @pl.when(f == pl.num_programs(2) - 1)
    def _():
        def emit(c, carry):
            yc_ref[pl.ds(pl.multiple_of(c * ch, ch), ch), :] = yacc_ref[c].astype(BF16)
            return carry

        def clear(c, carry):
            yc_ref[pl.ds(pl.multiple_of(c * ch, ch), ch), :] = jnp.zeros((ch, yc_ref.shape[1]), BF16)
            return carry

        lax.fori_loop(0, nch, emit, 0)
        lax.fori_loop(nch, yc_ref.shape[0] // ch, clear, 0)


def _moe_expert_call(nch, slot_exp, xn, w1, w3, w2, *, tm, tf, ch):
    n, d = xn.shape
    ne, _, dff = w1.shape
    nt = n // tm
    nch_max = -(-tm // ch)
    kern = functools.partial(_moe_expert_kernel, ch=ch)
    return pl.pallas_call(
        kern,
        grid_spec=pltpu.PrefetchScalarGridSpec(
            num_scalar_prefetch=1,
            grid=(ne, nt, dff // tf),
            in_specs=[pl.BlockSpec((None, ne, tm), lambda e, t, f, nch: (t, 0, 0)),
                      pl.BlockSpec((tm, d), lambda e, t, f, nch: (t, 0)),
                      pl.BlockSpec((None, d, tf), lambda e, t, f, nch: (e, 0, f)),
                      pl.BlockSpec((None, d, tf), lambda e, t, f, nch: (e, 0, f)),
                      pl.BlockSpec((None, tf, d), lambda e, t, f, nch: (e, f, 0))],
            out_specs=pl.BlockSpec((None, None, nch_max * ch, d), lambda e, t, f, nch: (e, t, 0, 0)),
            scratch_shapes=[pltpu.VMEM((nch_max, ch, d), BF16), pltpu.VMEM((nch_max, ch, d), F32)]),
        out_shape=jax.ShapeDtypeStruct((ne, nt, nch_max * ch, d), BF16),
        compiler_params=_cparams(("parallel", "parallel", "arbitrary")),
        name="moe_experts",
    )(nch, slot_exp, xn, w1, w3, w2)


def _moe_combine_kernel(nch_ref, x_ref, slot_ref, gate_ref, yc_ref, g_ref, o_ref, *, ch, final_norm):
    t = pl.program_id(0)
    e = pl.program_id(1)
    nch = nch_ref[t * pl.num_programs(1) + e]

    @pl.when(e == 0)
    def _():
        o_ref[...] = x_ref[...]

    lane = lax.broadcasted_iota(I32, slot_ref.shape, 1)
    scol = jnp.sum(jnp.where(lane == e, slot_ref[...], 0.0), axis=1, keepdims=True)
    gcol = jnp.sum(jnp.where(lane == e, gate_ref[...], 0.0), axis=1, keepdims=True)

    def scatter(c, carry):
        cols = (c * ch + lax.broadcasted_iota(I32, (1, ch), 1)).astype(F32)
        onehot = jnp.where(scol == cols, 1.0, 0.0).astype(BF16)
        yc = yc_ref[pl.ds(pl.multiple_of(c * ch, ch), ch), :]
        o_ref[...] += gcol * jnp.dot(onehot, yc, preferred_element_type=F32)
        return carry

    lax.fori_loop(0, nch, scatter, 0)

    if final_norm:
        @pl.when(e == pl.num_programs(1) - 1)
        def _():
            o_ref[...] = _rms(o_ref[...], g_ref[...])


def _moe_combine_call(nch, x, slot_tok, gate, yc, g_final, *, tm, ch, final_norm):
    n, d = x.shape
    ne, nt, rows, _ = yc.shape
    kern = functools.partial(_moe_combine_kernel, ch=ch, final_norm=final_norm)
    return pl.pallas_call(
        kern,
        grid_spec=pltpu.PrefetchScalarGridSpec(
            num_scalar_prefetch=1,
            grid=(nt, ne),
            in_specs=[pl.BlockSpec((tm, d), lambda t, e, nch: (t, 0)),
                      pl.BlockSpec((tm, LANES), lambda t, e, nch: (t, 0)),
                      pl.BlockSpec((tm, LANES), lambda t, e, nch: (t, 0)),
                      pl.BlockSpec((None, None, rows, d), lambda t, e, nch: (e, t, 0, 0)),
                      pl.BlockSpec((1, d), lambda t, e, nch: (0, 0))],
            out_specs=pl.BlockSpec((tm, d), lambda t, e, nch: (t, 0))),
        out_shape=jax.ShapeDtypeStruct((n, d), F32),
        compiler_params=_cparams(("parallel", "arbitrary")),
        name="moe_combine",
    )(nch, x, slot_tok, gate, yc, g_final)


def _moe_routing_tables(slot_tok, counts, *, tm, ch):
    n = slot_tok.shape[0]
    nt = n // tm
    cnt = counts[:, 0, :N_EXPERTS].astype(I32)
    nch = ((cnt + ch - 1) // ch).reshape(nt * N_EXPERTS)
    slot_exp = slot_tok[:, :N_EXPERTS].astype(I32).reshape(nt, tm, N_EXPERTS).transpose(0, 2, 1)
    return nch, slot_exp


def _norm_kernel(x_ref, g_ref, o_ref):
    o_ref[...] = _rms(x_ref[...], g_ref[...])


def _norm_call(x, g, *, tm):
    n, d = x.shape
    return pl.pallas_call(
        _norm_kernel,
        grid=(n // tm,),
        in_specs=[pl.BlockSpec((tm, d), lambda i: (i, 0)), pl.BlockSpec((1, d), lambda i: (0, 0))],
        out_specs=pl.BlockSpec((tm, d), lambda i: (i, 0)),
        out_shape=jax.ShapeDtypeStruct((n, d), F32),
        compiler_params=_cparams(("parallel",)),
        name="final_norm",
    )(x, g)


def _pack_w_in(w):
    d = w.shape[0]
    o = np.cumsum([0, 512, 128, 128, 256, 64, 4, 512, 768, 8, 512, 512, 3072])
    seg = lambda k: w[:, int(o[k]):int(o[k + 1])]
    q, k, v, qi, ki, wi, z, xbc, dt, xl, gl, gates = [seg(t) for t in range(12)]
    xs, bc = xbc[:, :MIX], xbc[:, MIX:]
    dt_exp = jnp.repeat(dt, SSM_HEAD_DIM, axis=1)
    pieces = ((COL_Q, q * (HEAD_DIM ** -0.5 * LOG2_E)), (COL_Z, z), (COL_XL, xl), (COL_GL, gl),
              (COL_GATES, gates), (COL_XS, xs), (COL_DT, dt_exp), (COL_BC, bc), (COL_QI, qi),
              (COL_K, k), (COL_V, v), (COL_KIWI, ki), (COL_KIWI + IDX_DIM, wi))
    packed = jnp.zeros((d, PROJ_W), BF16)
    for col, piece in pieces:
        packed = lax.dynamic_update_slice(packed, piece.astype(BF16), (0, col))
    return packed


def _block_diag(w):
    nblk, bw, _ = w.shape
    eye = jnp.eye(nblk, dtype=w.dtype)
    return jnp.einsum('kij,kl->kilj', w, eye).reshape(nblk * bw, nblk * bw)


def _rope_tables(pos):
    half = HEAD_DIM // 2
    inv = 1.0 / (ROPE_THETA ** (jnp.arange(half, dtype=F32) / half))
    ang = pos.astype(F32)[:, None] * inv[None, :]
    cos, sin = jnp.cos(ang), jnp.sin(ang)
    cos_t = jnp.concatenate([cos, cos, cos, cos], axis=1)
    sin_t = jnp.concatenate([-sin, sin, -sin, sin], axis=1)
    return cos_t, sin_t


def _pad_rows8(a):
    return jnp.pad(a, ((0, 0), (5, 0), (0, 0)))


def _state_to_s2(h):
    nb = h.shape[0]
    hg = h.reshape(nb, SSM_GROUPS, SSM_HEADS // SSM_GROUPS, SSM_HEAD_DIM, SSM_STATE)
    eye = jnp.eye(SSM_GROUPS, dtype=h.dtype)
    s2 = jnp.einsum('bgkpn,gf->bfngkp', hg, eye)
    return s2.reshape(nb, SSM_GROUPS * SSM_STATE, MIX)


def _s2_to_state(s2):
    nb = s2.shape[0]
    s6 = s2.reshape(nb, SSM_GROUPS, SSM_STATE, SSM_GROUPS, SSM_HEADS // SSM_GROUPS, SSM_HEAD_DIM)
    diag = jnp.stack([s6[:, g, :, g] for g in range(SSM_GROUPS)], axis=1)
    return diag.transpose(0, 1, 3, 4, 2).reshape(nb, SSM_HEADS, SSM_HEAD_DIM, SSM_STATE)


def _with_ones_column(v):
    n = v.shape[0]
    e = jnp.zeros((n, LANES - HEAD_DIM), v.dtype).at[:, 0].set(1)
    return jnp.concatenate([v[:, :HEAD_DIM], e, v[:, HEAD_DIM:], e], axis=1)


def _expand_heads(v):
    return jnp.repeat(v, SSM_HEAD_DIM)[None, :]


def kernel(x_prompt, x_sample, cache_k, cache_v, cache_kidx, state_ssm, state_ssm_conv, state_lru,
           state_lru_conv, norm_mix, norm_ffn, norm_final, w_in, ssm_conv_w, ssm_conv_b, ssm_dt_bias,
           ssm_a_log, ssm_d, ssm_norm, lru_conv_w, lru_conv_b, lru_wa, lru_ba, lru_wx, lru_bx, lru_lambda,
           w_att_out, w_ssm_out, w_lru_out, w_o, ffn_w1, ffn_w3, ffn_w2, moe_router, moe_w1, moe_w3, moe_w2):
    pb, pl_len, d = x_prompt.shape
    sb, sl_len, _ = x_sample.shape
    depth = w_in.shape[0]
    past = cache_k.shape[2]
    n_p = pb * pl_len
    n_s = sb * sl_len
    x = jnp.concatenate([x_prompt.reshape(n_p, d), x_sample.reshape(n_s, d)], axis=0)

    topk_p = min(TOPK_MAX, pl_len // 4)
    s_tot = past + sl_len
    topk_s = min(TOPK_MAX, s_tot // 4)
    kb_s = ATT_KB
    s_pad = -(-s_tot // kb_s) * kb_s

    cos_p, sin_p = _rope_tables(jnp.arange(pl_len))
    cos_s, sin_s = _rope_tables(past + jnp.arange(sl_len))

    groups = (
        dict(row0=0, nb=pb, length=pl_len),
        dict(row0=n_p, nb=sb, length=sl_len),
    )
    tl_p, tl_s = SEQ_TL, sl_len

    collected = [[[] for _ in range(7)] for _ in range(2)]
    final_fused = False
    for layer in range(depth):
        proj = _norm_matmul(x, norm_mix[layer][None, :], _pack_w_in(w_in[layer]), tm=IN_PROJ_TM, tn=IN_PROJ_TN)

        ssm_w = (ssm_conv_w[layer][:, :MIX], ssm_conv_w[layer][:, MIX:],
                 ssm_conv_b[layer][None, :MIX], ssm_conv_b[layer][None, MIX:],
                 _expand_heads(ssm_dt_bias[layer]), _expand_heads(ssm_a_log[layer]),
                 _expand_heads(ssm_d[layer]), ssm_norm[layer][None, :])
        lru_w = (lru_conv_w[layer], lru_conv_b[layer][None, :],
                 _block_diag(lru_wa[layer]).astype(BF16), lru_ba[layer][None, :],
                 _block_diag(lru_wx[layer]).astype(BF16), lru_bx[layer][None, :],
                 lru_lambda[layer][None, :])

        branch = [[], [], []]
        for gi, grp in enumerate(groups):
            nb, length = grp['nb'], grp['length']
            if gi == 0:
                tl, cos, sin = tl_p, cos_p, sin_p
                cx0 = jnp.zeros((nb, 8, MIX), F32)
                cbc0 = jnp.zeros((nb, 8, SSM_BC), F32)
                s0 = jnp.zeros((nb, 2 * SSM_STATE, MIX), F32)
                lc0 = jnp.zeros((nb, 8, MIX), F32)
                lh0 = jnp.zeros((nb, 1, MIX), F32)
            else:
                tl, cos, sin = tl_s, cos_s, sin_s
                conv0 = _pad_rows8(state_ssm_conv[layer])
                cx0, cbc0 = conv0[:, :, :MIX], conv0[:, :, MIX:]
                s0 = _state_to_s2(state_ssm[layer])
                lc0 = _pad_rows8(state_lru_conv[layer])
                lh0 = state_lru[layer][:, None, :]

            q_r, qi_r, k_r, v_r, ki_r, k_b, v_b, ki_b = _rope_call(proj, cos, sin, tl=tl, **grp)
            if gi == 0:
                att = _attn_call(q_r, qi_r, proj, k_b, _with_ones_column(v_b), ki_b, s_pad=length,
                                 s_valid=length, q_off=0, tq=ATT_TQ, kb=ATT_KB, topk=topk_p, **grp)
            else:
                def cat(cache, new, width):
                    c = cache.reshape(nb, past, width).astype(BF16)
                    a = jnp.concatenate([c, new.reshape(nb, length, width)], axis=1)
                    a = jnp.pad(a, ((0, 0), (0, s_pad - s_tot), (0, 0)))
                    return a.reshape(nb * s_pad, width)
                att = _attn_call(q_r, qi_r, proj, cat(cache_k[layer], k_b, 128),
                                 _with_ones_column(cat(cache_v[layer], v_b, 128)),
                                 cat(cache_kidx[layer], ki_b, IDX_DIM), s_pad=s_pad, s_valid=s_tot,
                                 q_off=past, tq=length, kb=kb_s, topk=topk_s, **grp)
            y_ssm, s_t, cx_t, cbc_t = _ssm_call(proj, cx0, cbc0, s0, ssm_w, tl=tl, **grp)
            y_lru, lh_t, lc_t = _lru_call(proj, lc0, lh0, lru_w, tl=tl, **grp)
            branch[0].append(att)
            branch[1].append(y_ssm)
            branch[2].append(y_lru)

            st = (k_r.reshape(nb, length, KV_HEADS, HEAD_DIM), v_r.reshape(nb, length, KV_HEADS, HEAD_DIM),
                  ki_r.reshape(nb, length, IDX_DIM), _s2_to_state(s_t),
                  jnp.concatenate([cx_t[:, 5:], cbc_t[:, 5:]], axis=2), lh_t[:, 0], lc_t[:, 5:])
            for lst, s in zip(collected[gi], st):
                lst.append(s)

        x = _merge_call(x, [bl[0] for bl in branch], [bl[1] for bl in branch], proj,
                        w_att_out[layer].astype(BF16),
                        w_ssm_out[layer].astype(BF16), w_lru_out[layer].astype(BF16),
                        w_o[layer].astype(BF16), tm=MERGE_TM)
        jl = layer // 2
        gf = norm_ffn[layer][None, :]
        if layer % 2 == 0:
            x = _ffn_call(x, gf, ffn_w1[jl].astype(BF16), ffn_w3[jl].astype(BF16),
                          ffn_w2[jl].astype(BF16), tm=FFN_TM, tf=FFN_TF)
        else:
            wr = jnp.pad(moe_router[jl], ((0, 0), (0, LANES - N_EXPERTS)))
            gate, xn, slot_tok, counts = _router_call(x, gf, wr, tm=ROW_TM, moe_tm=MOE_TM)
            nch, slot_exp = _moe_routing_tables(slot_tok, counts, tm=MOE_TM, ch=MOE_CH)
            yc = _moe_expert_call(nch, slot_exp, xn, moe_w1[jl].astype(BF16), moe_w3[jl].astype(BF16),
                                  moe_w2[jl].astype(BF16), tm=MOE_TM, tf=FFN_TF, ch=MOE_CH)
            final_fused = layer == depth - 1
            x = _moe_combine_call(nch, x, slot_tok, gate, yc, norm_final[None, :], tm=MOE_TM,
                                  ch=MOE_CH, final_norm=final_fused)

    y = x if final_fused else _norm_call(x, norm_final[None, :], tm=ROW_TM)
    y_prompt = y[:n_p].reshape(pb, pl_len, d)
    y_sample = y[n_p:].reshape(sb, sl_len, d)
    p_states = [jnp.stack(lst, axis=0) for lst in collected[0]]
    s_states = [jnp.stack(lst, axis=0) for lst in collected[1]]
    return (y_prompt, y_sample, *p_states, *s_states)
```
